```python
import math
import jax, jax.numpy as jnp
from jax import lax
import numpy as np

D_MODEL = 1024
BATCH = 16
SEQ = 2048
DEPTH = 4

GRID_W = 64
CTX_LEN = 256
N_MIXERS = 3
N_MODS = 9
EPS = 1e-6

D_FF = 2816

CONV_WIDTH = 31

D_INNER = 2 * D_MODEL
SSD_HEADDIM = 64
SSD_HEADS = D_INNER // SSD_HEADDIM
SSD_GROUPS = 8
SSD_HPG = SSD_HEADS // SSD_GROUPS
D_STATE = 128
SSD_CONV = 7
SSD_CHUNK = 128
SSD_GN = SSD_GROUPS * D_STATE
SSD_CONV_DIM = D_INNER + 2 * SSD_GN
SSD_PROJ = D_INNER + 2 * SSD_HEADS + SSD_CONV_DIM

CHUNK = 128
CMLP_E = 2 * D_MODEL
CMLP_GROUPS = 8
CMLP_GD = CMLP_E // CMLP_GROUPS

N_A = (DEPTH + 2) // 3
N_B = (DEPTH + 1) // 3
N_C = DEPTH // 3

kernel_name = "hybrid_conv_ssd_chunkmlp_diffusion_trunk"


def rmsnorm(x, w):
    xf = x.astype(jnp.float32)
    xf = xf * lax.rsqrt(jnp.mean(xf * xf, axis=-1, keepdims=True) + EPS)
    return xf.astype(x.dtype) * w


def layernorm(x, w, b):
    xf = x.astype(jnp.float32)
    mu = jnp.mean(xf, axis=-1, keepdims=True)
    var = jnp.mean(jnp.square(xf - mu), axis=-1, keepdims=True)
    return ((xf - mu) * lax.rsqrt(var + EPS)).astype(x.dtype) * w + b


def grouped_rmsnorm(y, w):
    shp = y.shape
    yf = y.astype(jnp.float32).reshape(shp[:-1] + (SSD_GROUPS, D_INNER // SSD_GROUPS))
    yf = yf * lax.rsqrt(jnp.mean(yf * yf, axis=-1, keepdims=True) + EPS)
    return yf.reshape(shp).astype(y.dtype) * w


def modulated_norm(s, gain, shift, scale):
    return rmsnorm(s, gain) * (1.0 + scale) + shift


def swiglu(h, w_in, w_out):
    g, u = jnp.split(h @ w_in, 2, axis=-1)
    return (jax.nn.silu(g) * u) @ w_out


def dwconv1d(u, k, b):
    y = lax.conv_general_dilated(u, k[:, None, :], window_strides=(1,), padding='SAME',
                                 dimension_numbers=('NWC', 'WIO', 'NWC'),
                                 feature_group_count=u.shape[-1])
    return y + b


def dwconv2d(u4, k4):
    return lax.conv_general_dilated(u4, k4, window_strides=(1, 1), padding='SAME',
                                    dimension_numbers=('NHWC', 'HWIO', 'NHWC'),
                                    feature_group_count=u4.shape[-1])


def conv_module(h, pw1_w, pw1_b, dw_w, dw_b, ln_w, ln_b, pw2_w, pw2_b, rows):
    a, g = jnp.split(h @ pw1_w + pw1_b, 2, axis=-1)
    u = a * jax.nn.sigmoid(g)
    if rows is None:
        u = dwconv1d(u, dw_w, dw_b)
    else:
        bsz, seq, d = u.shape
        half = d // 2
        u4 = u.reshape(bsz, rows, GRID_W, d)
        y_h = dwconv2d(u4[..., :half], dw_w[:, :half][None, :, None, :])
        y_v = dwconv2d(u4[..., half:], dw_w[:, half:][:, None, None, :])
        u = jnp.concatenate([y_h, y_v], axis=-1).reshape(bsz, seq, d) + dw_b
    u = jax.nn.silu(layernorm(u, ln_w, ln_b))
    return u @ pw2_w + pw2_b


def _ssd_chunks(xh, dt_raw, bm, dt_bias, a_log):
    bsz, seq = xh.shape[:2]
    nc = seq // SSD_CHUNK
    shp = (bsz, nc, SSD_CHUNK, SSD_GROUPS, SSD_HPG)
    dt = jax.nn.softplus(dt_raw.astype(jnp.float32) + dt_bias.astype(jnp.float32)).reshape(shp)
    a = -jnp.exp(a_log.astype(jnp.float32)).reshape(SSD_GROUPS, SSD_HPG)
    acs = jnp.cumsum(dt * a, axis=2)
    xg = xh.reshape(shp + (SSD_HEADDIM,))
    bg = bm.reshape(bsz, nc, SSD_CHUNK, SSD_GROUPS, D_STATE)
    return xg, dt, acs, bg


def _ssd_chunk_states(xg, dt, acs, bg, h0, collect):
    decay_end = jnp.exp(acs[:, :, -1:] - acs)
    st = jnp.einsum('bcqgn,bcqgr,bcqgrp->bcgrpn', bg, decay_end * dt, xg)
    chunk_decay = jnp.exp(acs[:, :, -1])

    def step(hc, inp):
        s_c, d_c = inp
        return d_c[..., None, None] * hc + s_c, (hc if collect else None)

    h_final, h_starts = lax.scan(step, h0, (jnp.moveaxis(st, 1, 0), jnp.moveaxis(chunk_decay, 1, 0)))
    return (jnp.moveaxis(h_starts, 0, 1) if collect else None), h_final


def ssd_scan(xh, dt_raw, bm, cm, dt_bias, a_log, h0):
    bsz, seq = xh.shape[:2]
    xg, dt, acs, bg = _ssd_chunks(xh, dt_raw, bm, dt_bias, a_log)
    cg = cm.reshape(bg.shape)
    h_starts, h_final = _ssd_chunk_states(xg, dt, acs, bg, h0, True)
    cb = jnp.einsum('bcign,bcjgn->bcgij', cg, bg)
    acs_t = jnp.moveaxis(acs, 2, -1)
    diff = acs_t[..., :, None] - acs_t[..., None, :]
    in_order = jnp.tril(jnp.ones((SSD_CHUNK, SSD_CHUNK), dtype=bool))
    lmat = jnp.exp(jnp.where(in_order, diff, -jnp.inf))
    y_diag = jnp.einsum('bcgij,bcgrij,bcjgr,bcjgrp->bcigrp', cb, lmat, dt, xg)
    y_off = jnp.einsum('bcign,bcgrpn,bcigr->bcigrp', cg, h_starts, jnp.exp(acs))
    y = (y_diag + y_off).reshape(bsz, seq, SSD_HEADS, SSD_HEADDIM)
    return y, h_final


def _flip(t):
    return jnp.flip(t, axis=1)


def ssd_bidir(h, in_w, conv_w, conv_b, dt_bias, a_log, d_skip, norm_w, out_w, h0_f, h0_b):
    bsz, seq = h.shape[:2]
    proj = h @ in_w
    z = proj[..., :D_INNER]
    dt_raw = proj[..., D_INNER:D_INNER + 2 * SSD_HEADS]
    xbc = jax.nn.silu(dwconv1d(proj[..., D_INNER + 2 * SSD_HEADS:], conv_w, conv_b))
    xh = xbc[..., :D_INNER].reshape(bsz, seq, SSD_HEADS, SSD_HEADDIM)
    bm = xbc[..., D_INNER:D_INNER + SSD_GN]
    cm = xbc[..., D_INNER + SSD_GN:]
    y_f, h_f = ssd_scan(xh, dt_raw[..., :SSD_HEADS], bm, cm, dt_bias[0], a_log[0], h0_f)
    y_b, h_b = ssd_scan(_flip(xh), _flip(dt_raw[..., SSD_HEADS:]), _flip(bm), _flip(cm),
                        dt_bias[1], a_log[1], h0_b)
    y = y_f + _flip(y_b) + d_skip[:, None] * xh
    y = y.reshape(bsz, seq, D_INNER).astype(h.dtype)
    y = grouped_rmsnorm(y * jax.nn.silu(z), norm_w)
    return y @ out_w, h_f, h_b


def ssd_context_states(hc, in_w, conv_w, conv_b, dt_bias, a_log):
    bsz, clen = hc.shape[:2]
    proj = hc @ in_w[:, D_INNER:D_INNER + 2 * SSD_HEADS + D_INNER + SSD_GN]
    dt_raw = proj[..., :2 * SSD_HEADS]
    xb = jax.nn.silu(dwconv1d(proj[..., 2 * SSD_HEADS:], conv_w[:, :D_INNER + SSD_GN],
                              conv_b[:D_INNER + SSD_GN]))
    xh = xb[..., :D_INNER].reshape(bsz, clen, SSD_HEADS, SSD_HEADDIM)
    bm = xb[..., D_INNER:]
    h0 = jnp.zeros((bsz, SSD_GROUPS, SSD_HPG, SSD_HEADDIM, D_STATE), jnp.float32)
    xg, dt, acs, bg = _ssd_chunks(xh, dt_raw[..., :SSD_HEADS], bm, dt_bias[0], a_log[0])
    _, h_f = _ssd_chunk_states(xg, dt, acs, bg, h0, False)
    xg, dt, acs, bg = _ssd_chunks(_flip(xh), _flip(dt_raw[..., SSD_HEADS:]), _flip(bm), dt_bias[1], a_log[1])
    _, h_b = _ssd_chunk_states(xg, dt, acs, bg, h0, False)
    return h_f, h_b


def chunk_mlp(h, in_w, in_b, ln_w, ln_b, sp_w, sp_b, out_w, out_b):
    bsz, seq, _ = h.shape
    z = jax.nn.gelu(h @ in_w + in_b, approximate=False)
    u, v = jnp.split(z, 2, axis=-1)
    v = layernorm(v, ln_w, ln_b).reshape(bsz, seq // CHUNK, CHUNK, CMLP_GROUPS, CMLP_GD)
    v = jnp.einsum('gij,bcjgd->bcigd', sp_w, v) + sp_b.T[:, :, None]
    return (u * v.reshape(bsz, seq, CMLP_E)) @ out_w + out_b


def setup_inputs(seed: int = 0) -> dict:
    key = jax.random.key(seed)
    ks = iter(jax.random.split(key, 40))

    def nrm(shape, scale):
        return jax.random.normal(next(ks), shape, jnp.float32) * scale

    def gain(shape):
        return 1.0 + nrm(shape, 0.1)

    d = D_MODEL
    x = nrm((BATCH, SEQ, d), 1.0)
    c = nrm((BATCH, d), 1.0)
    ctx = nrm((BATCH, CTX_LEN, d), 1.0)
    c_ctx = nrm((d,), 1.0)
    ada_w = nrm((DEPTH, d, N_MODS * d), 0.5 * d ** -0.5)
    ada_b = nrm((DEPTH, N_MODS * d), 0.01)
    norm_w = gain((DEPTH, 3, d))
    ffn_w_in = nrm((DEPTH, 2, d, 2 * D_FF), d ** -0.5)
    ffn_w_out = nrm((DEPTH, 2, D_FF, d), D_FF ** -0.5)
    conv_pw1_w = nrm((N_A, d, 2 * d), d ** -0.5)
    conv_pw1_b = nrm((N_A, 2 * d), 0.01)
    conv_dw_w = nrm((N_A, CONV_WIDTH, d), CONV_WIDTH ** -0.5)
    conv_dw_b = nrm((N_A, d), 0.01)
    conv_ln_w = gain((N_A, d))
    conv_ln_b = nrm((N_A, d), 0.01)
    conv_pw2_w = nrm((N_A, d, d), d ** -0.5)
    conv_pw2_b = nrm((N_A, d), 0.01)
    ssd_in_w = nrm((N_B, d, SSD_PROJ), d ** -0.5)
    ssd_conv_w = nrm((N_B, SSD_CONV, SSD_CONV_DIM), SSD_CONV ** -0.5)
    ssd_conv_b = nrm((N_B, SSD_CONV_DIM), 0.01)
    dt0 = jnp.exp(jax.random.uniform(next(ks), (N_B, 2, SSD_HEADS), jnp.float32,
                                     minval=math.log(1e-3), maxval=math.log(1e-1)))
    ssd_dt_bias = dt0 + jnp.log(-jnp.expm1(-dt0))
    ssd_a_log = jnp.log(jax.random.uniform(next(ks), (N_B, 2, SSD_HEADS), jnp.float32,
                                           minval=1.0, maxval=16.0))
    ssd_d = gain((N_B, SSD_HEADS))
    ssd_norm_w = gain((N_B, D_INNER))
    ssd_out_w = nrm((N_B, D_INNER, d), D_INNER ** -0.5)
    cmlp_in_w = nrm((N_C, d, 2 * CMLP_E), d ** -0.5)
    cmlp_in_b = nrm((N_C, 2 * CMLP_E), 0.01)
    cmlp_ln_w = gain((N_C, CMLP_E))
    cmlp_ln_b = nrm((N_C, CMLP_E), 0.01)
    cmlp_sp_w = nrm((N_C, CMLP_GROUPS, CHUNK, CHUNK), 0.5 * CHUNK ** -0.5)
    cmlp_sp_b = gain((N_C, CMLP_GROUPS, CHUNK))
    cmlp_out_w = nrm((N_C, CMLP_E, d), CMLP_E ** -0.5)
    cmlp_out_b = nrm((N_C, d), 0.01)
    final_norm_w = gain((d,))
    return {
        'x': x, 'c': c, 'ctx': ctx, 'c_ctx': c_ctx,
        'ada_w': ada_w, 'ada_b': ada_b, 'norm_w': norm_w,
        'ffn_w_in': ffn_w_in, 'ffn_w_out': ffn_w_out,
        'conv_pw1_w': conv_pw1_w, 'conv_pw1_b': conv_pw1_b, 'conv_dw_w': conv_dw_w, 'conv_dw_b': conv_dw_b,
        'conv_ln_w': conv_ln_w, 'conv_ln_b': conv_ln_b, 'conv_pw2_w': conv_pw2_w, 'conv_pw2_b': conv_pw2_b,
        'ssd_in_w': ssd_in_w, 'ssd_conv_w': ssd_conv_w, 'ssd_conv_b': ssd_conv_b,
        'ssd_dt_bias': ssd_dt_bias, 'ssd_a_log': ssd_a_log, 'ssd_d': ssd_d,
        'ssd_norm_w': ssd_norm_w, 'ssd_out_w': ssd_out_w,
        'cmlp_in_w': cmlp_in_w, 'cmlp_in_b': cmlp_in_b, 'cmlp_ln_w': cmlp_ln_w, 'cmlp_ln_b': cmlp_ln_b,
        'cmlp_sp_w': cmlp_sp_w, 'cmlp_sp_b': cmlp_sp_b, 'cmlp_out_w': cmlp_out_w, 'cmlp_out_b': cmlp_out_b,
        'final_norm_w': final_norm_w,
    }


def reference(x, c, ctx, c_ctx, ada_w, ada_b, norm_w, ffn_w_in, ffn_w_out,
              conv_pw1_w, conv_pw1_b, conv_dw_w, conv_dw_b, conv_ln_w, conv_ln_b, conv_pw2_w, conv_pw2_b,
              ssd_in_w, ssd_conv_w, ssd_conv_b, ssd_dt_bias, ssd_a_log, ssd_d, ssd_norm_w, ssd_out_w,
              cmlp_in_w, cmlp_in_b, cmlp_ln_w, cmlp_ln_b, cmlp_sp_w, cmlp_sp_b, cmlp_out_w, cmlp_out_b,
              final_norm_w):
    bsz, seq = x.shape[:2]
    rows = seq // GRID_W
    last_ssd = max((i for i in range(DEPTH) if i % N_MIXERS == 1), default=-1)
    x_ctx = ctx
    for i in range(DEPTH):
        kind, j = i % N_MIXERS, i // N_MIXERS
        ctx_live = i <= last_ssd
        ctx_carry = i < last_ssd
        ml = jnp.split((jax.nn.silu(c) @ ada_w[i] + ada_b[i])[:, None, :], N_MODS, axis=-1)
        mc = jnp.split(jax.nn.silu(c_ctx) @ ada_w[i] + ada_b[i], N_MODS, axis=-1) if ctx_live else None

        x = x + 0.5 * ml[2] * swiglu(modulated_norm(x, norm_w[i, 0], ml[0], ml[1]),
                                     ffn_w_in[i, 0], ffn_w_out[i, 0])
        if ctx_live:
            x_ctx = x_ctx + 0.5 * mc[2] * swiglu(modulated_norm(x_ctx, norm_w[i, 0], mc[0], mc[1]),
                                                 ffn_w_in[i, 0], ffn_w_out[i, 0])

        h = modulated_norm(x, norm_w[i, 1], ml[3], ml[4])
        hc = modulated_norm(x_ctx, norm_w[i, 1], mc[3], mc[4]) if ctx_live else None
        y_c = None
        if kind == 0:
            p = (conv_pw1_w[j], conv_pw1_b[j], conv_dw_w[j], conv_dw_b[j],
                 conv_ln_w[j], conv_ln_b[j], conv_pw2_w[j], conv_pw2_b[j])
            y = conv_module(h, *p, rows)
            if ctx_carry:
                y_c = conv_module(hc, *p, None)
        elif kind == 1:
            p = (ssd_in_w[j], ssd_conv_w[j], ssd_conv_b[j], ssd_dt_bias[j], ssd_a_log[j],
                 ssd_d[j], ssd_norm_w[j], ssd_out_w[j])
            if ctx_carry:
                h0 = jnp.zeros((bsz, SSD_GROUPS, SSD_HPG, SSD_HEADDIM, D_STATE), jnp.float32)
                y_c, hc_f, hc_b = ssd_bidir(hc, *p, h0, h0)
            else:
                hc_f, hc_b = ssd_context_states(hc, ssd_in_w[j], ssd_conv_w[j], ssd_conv_b[j],
                                                ssd_dt_bias[j], ssd_a_log[j])
            y, _, _ = ssd_bidir(h, *p, hc_f, hc_b)
        else:
            p = (cmlp_in_w[j], cmlp_in_b[j], cmlp_ln_w[j], cmlp_ln_b[j],
                 cmlp_sp_w[j], cmlp_sp_b[j], cmlp_out_w[j], cmlp_out_b[j])
            y = chunk_mlp(h, *p)
            if ctx_carry:
                y_c = chunk_mlp(hc, *p)
        x = x + ml[5] * y
        if ctx_carry:
            x_ctx = x_ctx + mc[5] * y_c

        x = x + 0.5 * ml[8] * swiglu(modulated_norm(x, norm_w[i, 2], ml[6], ml[7]),
                                     ffn_w_in[i, 1], ffn_w_out[i, 1])
        if ctx_carry:
            x_ctx = x_ctx + 0.5 * mc[8] * swiglu(modulated_norm(x_ctx, norm_w[i, 2], mc[6], mc[7]),
                                                 ffn_w_in[i, 1], ffn_w_out[i, 1])
    return rmsnorm(x, final_norm_w)
```

```python
import functools
import math

import jax
import jax.numpy as jnp
from jax import lax
from jax.experimental import pallas as pl
from jax.experimental.pallas import tpu as pltpu

D_MODEL = 1024
DEPTH = 4
GRID_W = 64
N_MIXERS = 3
N_MODS = 9
EPS = 1e-6
D_FF = 2816
CONV_WIDTH = 31
D_INNER = 2 * D_MODEL
SSD_HEADDIM = 64
SSD_HEADS = D_INNER // SSD_HEADDIM
SSD_GROUPS = 8
SSD_HPG = SSD_HEADS // SSD_GROUPS
D_STATE = 128
SSD_CONV = 7
SSD_CHUNK = 128
SSD_GN = SSD_GROUPS * D_STATE
SSD_CONV_DIM = D_INNER + 2 * SSD_GN
CHUNK = 128
CMLP_E = 2 * D_MODEL
CMLP_GROUPS = 8
CMLP_GD = CMLP_E // CMLP_GROUPS

SUBLANES = 8
VMEM_LIMIT_BYTES = 56 * 1024 * 1024

MOD_ROWS = 24
FFN_TM = 512
FFN_CHUNK = 256
BF16 = jnp.bfloat16
F32 = jnp.float32


def _silu(v):
    return v * jax.nn.sigmoid(v)


def _ada_kernel(c_ref, w_ref, b_ref, o_ref):
    s = _silu(c_ref[...]).astype(BF16)
    o_ref[...] = jnp.dot(s, w_ref[...].astype(BF16), preferred_element_type=F32) + b_ref[...]


def ada_modulation(cvec, ada_w, ada_b):
    depth, d, n = ada_w.shape
    tn = d
    return pl.pallas_call(
        _ada_kernel,
        grid=(depth, n // tn),
        in_specs=[
            pl.BlockSpec((MOD_ROWS, d), lambda l, j: (0, 0)),
            pl.BlockSpec((None, d, tn), lambda l, j: (l, 0, j)),
            pl.BlockSpec((None, 1, tn), lambda l, j: (l, 0, j)),
        ],
        out_specs=pl.BlockSpec((None, MOD_ROWS, tn), lambda l, j: (l, 0, j)),
        out_shape=jax.ShapeDtypeStruct((depth, MOD_ROWS, n), F32),
        compiler_params=pltpu.CompilerParams(
            dimension_semantics=("arbitrary", "arbitrary"), vmem_limit_bytes=VMEM_LIMIT_BYTES),
        name="ada_modulation",
    )(cvec, ada_w, ada_b.reshape(depth, 1, n))


def _ffn_kernel(x_ref, shift_ref, scale_ref, gate_ref, nw_ref, win_ref, wout_ref, o_ref, a_ref):
    x = x_ref[...]
    xn = x * lax.rsqrt(jnp.mean(x * x, axis=-1, keepdims=True) + EPS) * nw_ref[...]
    hb = (xn * (1.0 + scale_ref[...]) + shift_ref[...]).astype(BF16)
    for j in range(D_FF // FFN_CHUNK):
        lo = j * FFN_CHUNK
        g = jnp.dot(hb, win_ref[:, lo:lo + FFN_CHUNK], preferred_element_type=F32)
        u = jnp.dot(hb, win_ref[:, D_FF + lo:D_FF + lo + FFN_CHUNK], preferred_element_type=F32)
        a_ref[:, lo:lo + FFN_CHUNK] = (_silu(g) * u).astype(BF16)
    y = jnp.dot(a_ref[...], wout_ref[...], preferred_element_type=F32)
    o_ref[...] = x + (0.5 * gate_ref[...]) * y


def _mod_spec(m, tiles_per_batch, n_batch):
    return pl.BlockSpec(
        (None, 1, D_MODEL),
        lambda i: (jnp.minimum(i // tiles_per_batch, n_batch) * N_MODS + m, 0, 0))


def ffn_block(xs, n_tokens, mods, m0, nw, w_in, w_out, seq, n_batch):
    d = xs.shape[-1]
    tm = FFN_TM
    tpb = seq // tm
    const = lambda i: (0, 0)
    return pl.pallas_call(
        _ffn_kernel,
        grid=(n_tokens // tm,),
        in_specs=[
            pl.BlockSpec((tm, d), lambda i: (i, 0)),
            _mod_spec(m0, tpb, n_batch), _mod_spec(m0 + 1, tpb, n_batch), _mod_spec(m0 + 2, tpb, n_batch),
            pl.BlockSpec((1, d), const),
            pl.BlockSpec((d, 2 * D_FF), const, pipeline_mode=pl.Buffered(1)),
            pl.BlockSpec((D_FF, d), const, pipeline_mode=pl.Buffered(1)),
        ],
        out_specs=pl.BlockSpec((tm, d), lambda i: (i, 0)),
        out_shape=jax.ShapeDtypeStruct((n_tokens, d), F32),
        scratch_shapes=[pltpu.VMEM((tm, D_FF), BF16)],
        compiler_params=pltpu.CompilerParams(
            dimension_semantics=("arbitrary",), vmem_limit_bytes=VMEM_LIMIT_BYTES),
        name="ffn_block",
    )(xs, mods, mods, mods, nw.reshape(1, d), w_in, w_out)


def _rmsnorm(x, w):
    return x * lax.rsqrt(jnp.mean(x * x, axis=-1, keepdims=True) + EPS) * w


def _layernorm(x, w, b):
    mu = jnp.mean(x, axis=-1, keepdims=True)
    var = jnp.mean(jnp.square(x - mu), axis=-1, keepdims=True)
    return (x - mu) * lax.rsqrt(var + EPS) * w + b


def _grouped_rmsnorm(y, w):
    shp = y.shape
    yf = y.reshape(shp[:-1] + (SSD_GROUPS, D_INNER // SSD_GROUPS))
    yf = yf * lax.rsqrt(jnp.mean(yf * yf, axis=-1, keepdims=True) + EPS)
    return yf.reshape(shp) * w


def _dwconv1d(u, k, b):
    y = lax.conv_general_dilated(u, k[:, None, :], window_strides=(1,), padding='SAME',
                                 dimension_numbers=('NWC', 'WIO', 'NWC'), feature_group_count=u.shape[-1])
    return y + b


def _dwconv2d(u4, k4):
    return lax.conv_general_dilated(u4, k4, window_strides=(1, 1), padding='SAME',
                                    dimension_numbers=('NHWC', 'HWIO', 'NHWC'), feature_group_count=u4.shape[-1])


def _conv_module(h, pw1_w, pw1_b, dw_w, dw_b, ln_w, ln_b, pw2_w, pw2_b, rows):
    a, g = jnp.split(h @ pw1_w + pw1_b, 2, axis=-1)
    u = a * jax.nn.sigmoid(g)
    if rows is None:
        u = _dwconv1d(u, dw_w, dw_b)
    else:
        bsz, seq, d = u.shape
        half = d // 2
        u4 = u.reshape(bsz, rows, GRID_W, d)
        y_h = _dwconv2d(u4[..., :half], dw_w[:, :half][None, :, None, :])
        y_v = _dwconv2d(u4[..., half:], dw_w[:, half:][:, None, None, :])
        u = jnp.concatenate([y_h, y_v], axis=-1).reshape(bsz, seq, d) + dw_b
    u = jax.nn.silu(_layernorm(u, ln_w, ln_b))
    return u @ pw2_w + pw2_b


def _ssd_chunks(xh, dt_raw, bm, dt_bias, a_log):
    bsz, seq = xh.shape[:2]
    nc = seq // SSD_CHUNK
    shp = (bsz, nc, SSD_CHUNK, SSD_GROUPS, SSD_HPG)
    dt = jax.nn.softplus(dt_raw + dt_bias).reshape(shp)
    a = -jnp.exp(a_log).reshape(SSD_GROUPS, SSD_HPG)
    acs = jnp.cumsum(dt * a, axis=2)
    xg = xh.reshape(shp + (SSD_HEADDIM,))
    bg = bm.reshape(bsz, nc, SSD_CHUNK, SSD_GROUPS, D_STATE)
    return xg, dt, acs, bg


def _ssd_chunk_states(xg, dt, acs, bg, h0, collect):
    decay_end = jnp.exp(acs[:, :, -1:] - acs)
    st = jnp.einsum('bcqgn,bcqgr,bcqgrp->bcgrpn', bg, decay_end * dt, xg)
    chunk_decay = jnp.exp(acs[:, :, -1])

    def step(hc, inp):
        s_c, d_c = inp
        return d_c[..., None, None] * hc + s_c, (hc if collect else None)

    h_final, h_starts = lax.scan(step, h0, (jnp.moveaxis(st, 1, 0), jnp.moveaxis(chunk_decay, 1, 0)))
    return (jnp.moveaxis(h_starts, 0, 1) if collect else None), h_final


def _ssd_scan(xh, dt_raw, bm, cm, dt_bias, a_log, h0):
    bsz, seq = xh.shape[:2]
    xg, dt, acs, bg = _ssd_chunks(xh, dt_raw, bm, dt_bias, a_log)
    cg = cm.reshape(bg.shape)
    h_starts, h_final = _ssd_chunk_states(xg, dt, acs, bg, h0, True)
    cb = jnp.einsum('bcign,bcjgn->bcgij', cg, bg)
    acs_t = jnp.moveaxis(acs, 2, -1)
    diff = acs_t[..., :, None] - acs_t[..., None, :]
    in_order = jnp.tril(jnp.ones((SSD_CHUNK, SSD_CHUNK), dtype=bool))
    lmat = jnp.exp(jnp.where(in_order, diff, -jnp.inf))
    y_diag = jnp.einsum('bcgij,bcgrij,bcjgr,bcjgrp->bcigrp', cb, lmat, dt, xg)
    y_off = jnp.einsum('bcign,bcgrpn,bcigr->bcigrp', cg, h_starts, jnp.exp(acs))
    y = (y_diag + y_off).reshape(bsz, seq, SSD_HEADS, SSD_HEADDIM)
    return y, h_final


def _flip(t):
    return jnp.flip(t, axis=1)


def _ssd_bidir(h, in_w, conv_w, conv_b, dt_bias, a_log, d_skip, norm_w, out_w, h0_f, h0_b):
    bsz, seq = h.shape[:2]
    proj = h @ in_w
    z = proj[..., :D_INNER]
    dt_raw = proj[..., D_INNER:D_INNER + 2 * SSD_HEADS]
    xbc = jax.nn.silu(_dwconv1d(proj[..., D_INNER + 2 * SSD_HEADS:], conv_w, conv_b))
    xh = xbc[..., :D_INNER].reshape(bsz, seq, SSD_HEADS, SSD_HEADDIM)
    bm = xbc[..., D_INNER:D_INNER + SSD_GN]
    cm = xbc[..., D_INNER + SSD_GN:]
    y_f, h_f = _ssd_scan(xh, dt_raw[..., :SSD_HEADS], bm, cm, dt_bias[0], a_log[0], h0_f)
    y_b, h_b = _ssd_scan(_flip(xh), _flip(dt_raw[..., SSD_HEADS:]), _flip(bm), _flip(cm),
                         dt_bias[1], a_log[1], h0_b)
    y = y_f + _flip(y_b) + d_skip[:, None] * xh
    y = y.reshape(bsz, seq, D_INNER)
    y = _grouped_rmsnorm(y * jax.nn.silu(z), norm_w)
    return y @ out_w, h_f, h_b


def _ssd_context_states(hc, in_w, conv_w, conv_b, dt_bias, a_log):
    bsz, clen = hc.shape[:2]
    proj = hc @ in_w[:, D_INNER:D_INNER + 2 * SSD_HEADS + D_INNER + SSD_GN]
    dt_raw = proj[..., :2 * SSD_HEADS]
    xb = jax.nn.silu(_dwconv1d(proj[..., 2 * SSD_HEADS:], conv_w[:, :D_INNER + SSD_GN],
                               conv_b[:D_INNER + SSD_GN]))
    xh = xb[..., :D_INNER].reshape(bsz, clen, SSD_HEADS, SSD_HEADDIM)
    bm = xb[..., D_INNER:]
    h0 = jnp.zeros((bsz, SSD_GROUPS, SSD_HPG, SSD_HEADDIM, D_STATE), jnp.float32)
    xg, dt, acs, bg = _ssd_chunks(xh, dt_raw[..., :SSD_HEADS], bm, dt_bias[0], a_log[0])
    _, h_f = _ssd_chunk_states(xg, dt, acs, bg, h0, False)
    xg, dt, acs, bg = _ssd_chunks(_flip(xh), _flip(dt_raw[..., SSD_HEADS:]), _flip(bm), dt_bias[1], a_log[1])
    _, h_b = _ssd_chunk_states(xg, dt, acs, bg, h0, False)
    return h_f, h_b


def _chunk_mlp(h, in_w, in_b, ln_w, ln_b, sp_w, sp_b, out_w, out_b):
    bsz, seq, _ = h.shape
    z = jax.nn.gelu(h @ in_w + in_b, approximate=False)
    u, v = jnp.split(z, 2, axis=-1)
    v = _layernorm(v, ln_w, ln_b).reshape(bsz, seq // CHUNK, CHUNK, CMLP_GROUPS, CMLP_GD)
    v = jnp.einsum('gij,bcjgd->bcigd', sp_w, v) + sp_b.T[:, :, None]
    return (u * v.reshape(bsz, seq, CMLP_E)) @ out_w + out_b


def kernel(x, c, ctx, c_ctx, ada_w, ada_b, norm_w, ffn_w_in, ffn_w_out, conv_pw1_w, conv_pw1_b, conv_dw_w, conv_dw_b, conv_ln_w, conv_ln_b, conv_pw2_w, conv_pw2_b, ssd_in_w, ssd_conv_w, ssd_conv_b, ssd_dt_bias, ssd_a_log, ssd_d, ssd_norm_w, ssd_out_w, cmlp_in_w, cmlp_in_b, cmlp_ln_w, cmlp_ln_b, cmlp_sp_w, cmlp_sp_b, cmlp_out_w, cmlp_out_b, final_norm_w):
    bsz, seq, d = x.shape
    clen = ctx.shape[1]
    rows = seq // GRID_W
    n_lat = bsz * seq
    n_all = n_lat + bsz * clen
    last_ssd = max((i for i in range(DEPTH) if i % N_MIXERS == 1), default=-1)

    cvec = jnp.concatenate([c, c_ctx[None, :], jnp.zeros((MOD_ROWS - bsz - 1, d), F32)], axis=0)
    mods_all = ada_modulation(cvec, ada_w, ada_b)
    w_in_b = ffn_w_in.astype(BF16)
    w_out_b = ffn_w_out.astype(BF16)

    xs = jnp.concatenate([x.reshape(n_lat, d), ctx.reshape(bsz * clen, d)], axis=0)
    for i in range(DEPTH):
        kind, j = i % N_MIXERS, i // N_MIXERS
        ctx_live = i <= last_ssd
        ctx_carry = i < last_ssd
        mods = mods_all[i].reshape(MOD_ROWS * N_MODS, 1, d)
        ml = [mods_all[i, :bsz, m * d:(m + 1) * d][:, None, :] for m in range(N_MODS)]
        mc = [mods_all[i, bsz, m * d:(m + 1) * d] for m in range(N_MODS)]

        xs = ffn_block(xs, n_all if ctx_live else n_lat, mods, 0, norm_w[i, 0], w_in_b[i, 0], w_out_b[i, 0],
                       seq, bsz)

        xl = xs[:n_lat].reshape(bsz, seq, d)
        h = _rmsnorm(xl, norm_w[i, 1]) * (1.0 + ml[4]) + ml[3]
        if ctx_live:
            xc = xs[n_lat:].reshape(bsz, clen, d)
            hc = _rmsnorm(xc, norm_w[i, 1]) * (1.0 + mc[4]) + mc[3]
        y_c = None
        if kind == 0:
            p = (conv_pw1_w[j], conv_pw1_b[j], conv_dw_w[j], conv_dw_b[j],
                 conv_ln_w[j], conv_ln_b[j], conv_pw2_w[j], conv_pw2_b[j])
            y = _conv_module(h, *p, rows)
            if ctx_carry:
                y_c = _conv_module(hc, *p, None)
        elif kind == 1:
            p = (ssd_in_w[j], ssd_conv_w[j], ssd_conv_b[j], ssd_dt_bias[j], ssd_a_log[j],
                 ssd_d[j], ssd_norm_w[j], ssd_out_w[j])
            if ctx_carry:
                h0 = jnp.zeros((bsz, SSD_GROUPS, SSD_HPG, SSD_HEADDIM, D_STATE), jnp.float32)
                y_c, hc_f, hc_b = _ssd_bidir(hc, *p, h0, h0)
            else:
                hc_f, hc_b = _ssd_context_states(hc, ssd_in_w[j], ssd_conv_w[j], ssd_conv_b[j],
                                                 ssd_dt_bias[j], ssd_a_log[j])
            y, _, _ = _ssd_bidir(h, *p, hc_f, hc_b)
        else:
            p = (cmlp_in_w[j], cmlp_in_b[j], cmlp_ln_w[j], cmlp_ln_b[j],
                 cmlp_sp_w[j], cmlp_sp_b[j], cmlp_out_w[j], cmlp_out_b[j])
            y = _chunk_mlp(h, *p)
            if ctx_carry:
                y_c = _chunk_mlp(hc, *p)
        xl = xl + ml[5] * y
        if ctx_carry:
            xc = xc + mc[5] * y_c
            xs = jnp.concatenate([xl.reshape(n_lat, d), xc.reshape(bsz * clen, d)], axis=0)
        else:
            xs = xl.reshape(n_lat, d)

        xs = ffn_block(xs, n_all if ctx_carry else n_lat, mods, 6, norm_w[i, 2], w_in_b[i, 1], w_out_b[i, 1],
                       seq, bsz)
    out = _rmsnorm(xs, final_norm_w)
    return out.reshape(bsz, seq, d)
```

```python
import functools
import math

import jax
import jax.numpy as jnp
from jax import lax
from jax.experimental import pallas as pl
from jax.experimental.pallas import tpu as pltpu

D_MODEL = 1024
DEPTH = 4
GRID_W = 64
N_MIXERS = 3
N_MODS = 9
EPS = 1e-6
D_FF = 2816
CONV_WIDTH = 31
D_INNER = 2 * D_MODEL
SSD_HEADDIM = 64
SSD_HEADS = D_INNER // SSD_HEADDIM
SSD_GROUPS = 8
SSD_HPG = SSD_HEADS // SSD_GROUPS
D_STATE = 128
SSD_CONV = 7
SSD_CHUNK = 128
SSD_GN = SSD_GROUPS * D_STATE
SSD_CONV_DIM = D_INNER + 2 * SSD_GN
CHUNK = 128
CMLP_E = 2 * D_MODEL
CMLP_GROUPS = 8
CMLP_GD = CMLP_E // CMLP_GROUPS

SUBLANES = 8
VMEM_LIMIT_BYTES = 56 * 1024 * 1024

MOD_ROWS = 24
FFN_TM = 512
FFN_CHUNK = 256
BF16 = jnp.bfloat16
F32 = jnp.float32


def _silu(v):
    return v * jax.nn.sigmoid(v)


def _ada_kernel(c_ref, w_ref, b_ref, o_ref):
    s = _silu(c_ref[...]).astype(BF16)
    o_ref[...] = jnp.dot(s, w_ref[...].astype(BF16), preferred_element_type=F32) + b_ref[...]


def ada_modulation(cvec, ada_w, ada_b):
    depth, d, n = ada_w.shape
    tn = d
    return pl.pallas_call(
        _ada_kernel,
        grid=(depth, n // tn),
        in_specs=[
            pl.BlockSpec((MOD_ROWS, d), lambda l, j: (0, 0)),
            pl.BlockSpec((None, d, tn), lambda l, j: (l, 0, j)),
            pl.BlockSpec((None, 1, tn), lambda l, j: (l, 0, j)),
        ],
        out_specs=pl.BlockSpec((None, MOD_ROWS, tn), lambda l, j: (l, 0, j)),
        out_shape=jax.ShapeDtypeStruct((depth, MOD_ROWS, n), F32),
        compiler_params=pltpu.CompilerParams(
            dimension_semantics=("arbitrary", "arbitrary"), vmem_limit_bytes=VMEM_LIMIT_BYTES),
        name="ada_modulation",
    )(cvec, ada_w, ada_b.reshape(depth, 1, n))


def _ffn_kernel(x_ref, shift_ref, scale_ref, gate_ref, nw_ref, win_ref, wout_ref, o_ref, a_ref):
    x = x_ref[...]
    xn = x * lax.rsqrt(jnp.mean(x * x, axis=-1, keepdims=True) + EPS) * nw_ref[...]
    hb = (xn * (1.0 + scale_ref[...]) + shift_ref[...]).astype(BF16)
    for j in range(D_FF // FFN_CHUNK):
        lo = j * FFN_CHUNK
        g = jnp.dot(hb, win_ref[:, lo:lo + FFN_CHUNK], preferred_element_type=F32)
        u = jnp.dot(hb, win_ref[:, D_FF + lo:D_FF + lo + FFN_CHUNK], preferred_element_type=F32)
        a_ref[:, lo:lo + FFN_CHUNK] = (_silu(g) * u).astype(BF16)
    y = jnp.dot(a_ref[...], wout_ref[...], preferred_element_type=F32)
    o_ref[...] = x + (0.5 * gate_ref[...]) * y


def _mod_spec(m, tiles_per_batch, n_batch):
    return pl.BlockSpec(
        (None, 1, D_MODEL),
        lambda i: (jnp.minimum(i // tiles_per_batch, n_batch) * N_MODS + m, 0, 0))


def ffn_block(xs, n_tokens, mods, m0, nw, w_in, w_out, seq, n_batch):
    d = xs.shape[-1]
    tm = FFN_TM
    tpb = seq // tm
    const = lambda i: (0, 0)
    return pl.pallas_call(
        _ffn_kernel,
        grid=(n_tokens // tm,),
        in_specs=[
            pl.BlockSpec((tm, d), lambda i: (i, 0)),
            _mod_spec(m0, tpb, n_batch), _mod_spec(m0 + 1, tpb, n_batch), _mod_spec(m0 + 2, tpb, n_batch),
            pl.BlockSpec((1, d), const),
            pl.BlockSpec((d, 2 * D_FF), const, pipeline_mode=pl.Buffered(1)),
            pl.BlockSpec((D_FF, d), const, pipeline_mode=pl.Buffered(1)),
        ],
        out_specs=pl.BlockSpec((tm, d), lambda i: (i, 0)),
        out_shape=jax.ShapeDtypeStruct((n_tokens, d), F32),
        scratch_shapes=[pltpu.VMEM((tm, D_FF), BF16)],
        compiler_params=pltpu.CompilerParams(
            dimension_semantics=("arbitrary",), vmem_limit_bytes=VMEM_LIMIT_BYTES),
        name="ffn_block",
    )(xs, mods, mods, mods, nw.reshape(1, d), w_in, w_out)


def _proj_res_kernel(a_ref, x_ref, gate_ref, w_ref, b_ref, o_ref):
    y = jnp.dot(a_ref[...], w_ref[...], preferred_element_type=F32) + b_ref[...]
    o_ref[...] = x_ref[...] + gate_ref[...] * y


def proj_residual(a, xs, mods, m_gate, w, bias, seq, n_batch):
    n_tok, k = a.shape
    d = w.shape[1]
    tm = FFN_TM
    tpb = seq // tm
    const = lambda i: (0, 0)
    return pl.pallas_call(
        _proj_res_kernel,
        grid=(n_tok // tm,),
        in_specs=[
            pl.BlockSpec((tm, k), lambda i: (i, 0)),
            pl.BlockSpec((tm, d), lambda i: (i, 0)),
            _mod_spec(m_gate, tpb, n_batch),
            pl.BlockSpec((k, d), const, pipeline_mode=pl.Buffered(1)),
            pl.BlockSpec((1, d), const),
        ],
        out_specs=pl.BlockSpec((tm, d), lambda i: (i, 0)),
        out_shape=jax.ShapeDtypeStruct((n_tok, d), F32),
        compiler_params=pltpu.CompilerParams(
            dimension_semantics=("arbitrary",), vmem_limit_bytes=VMEM_LIMIT_BYTES),
        name="proj_residual",
    )(a, xs, mods, w, bias.reshape(1, d))


SSD_HALO = SUBLANES
SSD_NEG = -1e30
SSD_XBC_CHUNK = 512
HP = D_INNER
GHP = D_INNER // SSD_GROUPS


def _softplus(v):
    return jnp.maximum(v, 0.0) + jnp.log1p(jnp.exp(-jnp.abs(v)))


def _modnorm(x, nw, shift, scale):
    xn = x * lax.rsqrt(jnp.mean(x * x, axis=-1, keepdims=True) + EPS) * nw
    return xn * (1.0 + scale) + shift


def _ssd_proj_kernel(xp_ref, x_ref, xn_ref, shift_ref, scale_ref, nw_ref, wz_ref, wdt_ref, wxbc_ref,
                     cw_ref, cb_ref, dtb_ref, z_ref, dt_ref, xbc_ref, ext_ref, *, tiles_per_seq, tm):
    pos = pl.program_id(0) % tiles_per_seq
    nw, shift, scale = nw_ref[...], shift_ref[...], scale_ref[...]
    h = _modnorm(x_ref[...], nw, shift, scale)
    hp = _modnorm(xp_ref[...], nw, shift, scale) * (pos > 0).astype(F32)
    hn = _modnorm(xn_ref[...], nw, shift, scale) * (pos < tiles_per_seq - 1).astype(F32)
    hb = h.astype(BF16)
    hext = jnp.concatenate([hp, h, hn], axis=0).astype(BF16)

    for j in range(D_INNER // SSD_XBC_CHUNK):
        cols = slice(j * SSD_XBC_CHUNK, (j + 1) * SSD_XBC_CHUNK)
        z_ref[:, cols] = jnp.dot(hb, wz_ref[:, cols], preferred_element_type=F32).astype(BF16)
    dt_ref[...] = _softplus(jnp.dot(hb, wdt_ref[...], preferred_element_type=F32) + dtb_ref[...])

    off = SSD_HALO - SSD_CONV // 2
    for j in range(SSD_CONV_DIM // SSD_XBC_CHUNK):
        cols = slice(j * SSD_XBC_CHUNK, (j + 1) * SSD_XBC_CHUNK)
        ext_ref[:, cols] = jnp.dot(hext, wxbc_ref[:, cols], preferred_element_type=F32)
        acc = ext_ref[pl.ds(off, tm), cols] * cw_ref[0:1, cols] + cb_ref[:, cols]
        for k in range(1, SSD_CONV):
            acc = acc + ext_ref[pl.ds(off + k, tm), cols] * cw_ref[k:k + 1, cols]
        xbc_ref[:, cols] = _silu(acc).astype(BF16)


def ssd_proj(xs, tok0, n_tok, seq, tm, mods, mod_row, nw, w_z, w_dt, w_xbc, conv_w, conv_b, dt_bias128):
    d = xs.shape[-1]
    tps = seq // tm
    t0 = tok0 // tm
    hb = tm // SSD_HALO
    n_hblk = xs.shape[0] // SSD_HALO
    const = lambda i: (0, 0)
    mspec = lambda m: pl.BlockSpec((None, 1, d), lambda i: (mod_row(i) * N_MODS + m, 0, 0))
    return pl.pallas_call(
        functools.partial(_ssd_proj_kernel, tiles_per_seq=tps, tm=tm),
        grid=(n_tok // tm,),
        in_specs=[
            pl.BlockSpec((SSD_HALO, d), lambda i: (jnp.maximum((i + t0) * hb - 1, 0), 0)),
            pl.BlockSpec((tm, d), lambda i: (i + t0, 0)),
            pl.BlockSpec((SSD_HALO, d), lambda i: (jnp.minimum((i + t0 + 1) * hb, n_hblk - 1), 0)),
            mspec(3), mspec(4),
            pl.BlockSpec((1, d), const),
            pl.BlockSpec((d, D_INNER), const, pipeline_mode=pl.Buffered(1)),
            pl.BlockSpec((d, 128), const, pipeline_mode=pl.Buffered(1)),
            pl.BlockSpec((d, SSD_CONV_DIM), const, pipeline_mode=pl.Buffered(1)),
            pl.BlockSpec((SSD_CONV + 1, SSD_CONV_DIM), const),
            pl.BlockSpec((1, SSD_CONV_DIM), const),
            pl.BlockSpec((1, 128), const),
        ],
        out_specs=[
            pl.BlockSpec((tm, D_INNER), lambda i: (i, 0)),
            pl.BlockSpec((tm, 128), lambda i: (i, 0)),
            pl.BlockSpec((tm, SSD_CONV_DIM), lambda i: (i, 0)),
        ],
        out_shape=[
            jax.ShapeDtypeStruct((n_tok, D_INNER), BF16),
            jax.ShapeDtypeStruct((n_tok, 128), F32),
            jax.ShapeDtypeStruct((n_tok, SSD_CONV_DIM), BF16),
        ],
        scratch_shapes=[pltpu.VMEM((tm + 2 * SSD_HALO, SSD_CONV_DIM), F32)],
        compiler_params=pltpu.CompilerParams(
            dimension_semantics=("arbitrary",), vmem_limit_bytes=VMEM_LIMIT_BYTES),
        name="ssd_proj",
    )(xs, xs, xs, mods, mods, nw.reshape(1, d), w_z, w_dt, w_xbc,
      jnp.concatenate([conv_w, jnp.zeros((1, SSD_CONV_DIM), F32)], axis=0), conv_b.reshape(1, SSD_CONV_DIM),
      dt_bias128)


def _split3(v):
    hi = v.astype(BF16)
    r = v - hi.astype(F32)
    mid = r.astype(BF16)
    lo = (r - mid.astype(F32)).astype(BF16)
    return hi, mid, lo


def _dot3_left(tri, v):
    hi, mid, lo = _split3(v)
    return (jnp.dot(tri, hi, preferred_element_type=F32) + jnp.dot(tri, mid, preferred_element_type=F32)
            + jnp.dot(tri, lo, preferred_element_type=F32))


def _dot3_right(v, tri):
    hi, mid, lo = _split3(v)
    return (jnp.dot(hi, tri, preferred_element_type=F32) + jnp.dot(mid, tri, preferred_element_type=F32)
            + jnp.dot(lo, tri, preferred_element_type=F32))


def _chunk_decays(dt128, alog128, with_rows):
    q = dt128.shape[0]
    ri = lax.broadcasted_iota(jnp.int32, (q, q), 0)
    ci = lax.broadcasted_iota(jnp.int32, (q, q), 1)
    ltri = (ci <= ri).astype(BF16)
    utri = (ci >= ri).astype(BF16)
    lane = lax.broadcasted_iota(jnp.int32, (1, 128), 1)
    lane_fwd = (lane % 64) < 32
    v = dt128 * (-jnp.exp(alog128))
    csl = _dot3_left(ltri, v)
    csu = _dot3_left(utri, v)
    cs = jnp.where(lane_fwd, csl, csu)
    edge = jnp.where(lane_fwd, csl[q - 1:q, :], csu[0:1, :])
    qm = jnp.where(lane < 64, jnp.exp(cs), jnp.exp(edge - cs) * dt128)
    if not with_rows:
        return cs, edge, qm, None, None
    vt = v.T
    row = lax.broadcasted_iota(jnp.int32, (128, 1), 0)
    row_fwd = (row % 64) < 32
    cst = jnp.where(row_fwd, _dot3_right(vt, utri), _dot3_right(vt, ltri))
    return cs, edge, qm, cst, dt128.T


def _expand_heads(qm, e_mat):
    hi = qm.astype(BF16)
    lo = (qm - hi.astype(F32)).astype(BF16)
    return jnp.dot(jnp.concatenate([hi, lo], axis=1), e_mat, preferred_element_type=F32)


def _group_state(bg, xg, wg):
    xw = (xg.astype(F32) * wg).astype(BF16)
    return lax.dot_general(bg, xw, (((0,), (0,)), ((), ())), preferred_element_type=F32)


def _ssd_states_kernel(xf_ref, bf_ref, dtf_ref, xb_ref, bb_ref, dtb_ref, alog_ref, e_ref, hf_ref, hb_ref):
    s = pl.program_id(1)

    @pl.when(s == 0)
    def _():
        hf_ref[...] = jnp.zeros_like(hf_ref)
        hb_ref[...] = jnp.zeros_like(hb_ref)

    for x_ref, b_ref, dt_ref, st_ref, k_w, k_dec, edge_row in (
            (xf_ref, bf_ref, dtf_ref, hf_ref, 2, 0, SSD_CHUNK - 1), (xb_ref, bb_ref, dtb_ref, hb_ref, 3, 1, 0)):
        _, _, qm, _, _ = _chunk_decays(dt_ref[...], alog_ref[...], False)
        w = _expand_heads(qm, e_ref[k_w])
        dec = _expand_heads(qm, e_ref[k_dec])[edge_row:edge_row + 1, :]
        x = x_ref[...]
        bm = b_ref[...]
        for g in range(SSD_GROUPS):
            gl = slice(g * GHP, (g + 1) * GHP)
            st = _group_state(bm[:, g * D_STATE:(g + 1) * D_STATE], x[:, gl], w[:, gl])
            st_ref[g] = st_ref[g] * dec[:, gl] + st


def ssd_context_states(xbc, dt, alog128, e_all, n_batch, seq):
    nc = seq // SSD_CHUNK
    q = SSD_CHUNK
    st_shape = jax.ShapeDtypeStruct((n_batch, SSD_GROUPS, D_STATE, GHP), F32)
    st_spec = pl.BlockSpec((None, SSD_GROUPS, D_STATE, GHP), lambda b, s: (b, 0, 0, 0))
    fwd = lambda b, s: b * nc + s
    bwd = lambda b, s: b * nc + nc - 1 - s
    return pl.pallas_call(
        _ssd_states_kernel,
        grid=(n_batch, nc),
        in_specs=[
            pl.BlockSpec((q, HP), lambda b, s: (fwd(b, s), 0)),
            pl.BlockSpec((q, SSD_GN), lambda b, s: (fwd(b, s), HP // SSD_GN)),
            pl.BlockSpec((q, 128), lambda b, s: (fwd(b, s), 0)),
            pl.BlockSpec((q, HP), lambda b, s: (bwd(b, s), 0)),
            pl.BlockSpec((q, SSD_GN), lambda b, s: (bwd(b, s), HP // SSD_GN)),
            pl.BlockSpec((q, 128), lambda b, s: (bwd(b, s), 0)),
            pl.BlockSpec((1, 128), lambda b, s: (0, 0)),
            pl.BlockSpec((4, 256, HP), lambda b, s: (0, 0, 0), pipeline_mode=pl.Buffered(1)),
        ],
        out_specs=[st_spec, st_spec],
        out_shape=[st_shape, st_shape],
        compiler_params=pltpu.CompilerParams(
            dimension_semantics=("arbitrary", "arbitrary"), vmem_limit_bytes=VMEM_LIMIT_BYTES),
        name="ssd_context_states",
    )(xbc, xbc, dt, xbc, xbc, dt, alog128, e_all)


def _ssd_scan_kernel(x_ref, b_ref, c_ref, z_ref, dt_ref, alog_ref, dskip_ref, normw_ref, e_ref, h0f_ref, h0b_ref,
                     o_ref, hf_ref, hb_ref, hbs_ref, *, nc):
    s = pl.program_id(1)
    q = SSD_CHUNK

    @pl.when(s == 0)
    def _():
        hf_ref[...] = h0f_ref[...]
        hb_ref[...] = h0b_ref[...]

    @pl.when(s < nc)
    def _():
        cc = nc - 1 - s
        _, _, qm, _, _ = _chunk_decays(dt_ref[...], alog_ref[...], False)
        w = _expand_heads(qm, e_ref[3])
        dec = _expand_heads(qm, e_ref[1])[0:1, :]
        x = x_ref[...]
        bm = b_ref[...]
        for g in range(SSD_GROUPS):
            gl = slice(g * GHP, (g + 1) * GHP)
            hbs_ref[cc, g] = hb_ref[g].astype(BF16)
            st = _group_state(bm[:, g * D_STATE:(g + 1) * D_STATE], x[:, gl], w[:, gl])
            hb_ref[g] = hb_ref[g] * dec[:, gl] + st

    @pl.when(s >= nc)
    def _():
        cc = s - nc
        cs, _, qm, cst, dtt = _chunk_decays(dt_ref[...], alog_ref[...], True)
        ef = _expand_heads(qm, e_ref[0])
        eb = _expand_heads(qm, e_ref[1])
        wf = _expand_heads(qm, e_ref[2])
        dec = ef[q - 1:q, :]
        x = x_ref[...]
        bm = b_ref[...]
        cm = c_ref[...]
        ri = lax.broadcasted_iota(jnp.int32, (q, q), 0)
        ci = lax.broadcasted_iota(jnp.int32, (q, q), 1)
        tril = ci <= ri
        triu = ci >= ri
        glane = lax.broadcasted_iota(jnp.int32, (1, GHP), 1) // SSD_HEADDIM
        for g in range(SSD_GROUPS):
            gl = slice(g * GHP, (g + 1) * GHP)
            bg = bm[:, g * D_STATE:(g + 1) * D_STATE]
            cg = cm[:, g * D_STATE:(g + 1) * D_STATE]
            xg = x[:, gl]
            cb = lax.dot_general(cg, bg, (((1,), (1,)), ((), ())), preferred_element_type=F32)
            ms, xs = [], []
            for r in range(SSD_HPG):
                hd = g * SSD_HPG + r
                hb_ = SSD_HEADS + hd
                lf = jnp.exp(jnp.where(tril, cs[:, hd:hd + 1] - cst[hd:hd + 1, :], SSD_NEG))
                lb = jnp.exp(jnp.where(triu, cs[:, hb_:hb_ + 1] - cst[hb_:hb_ + 1, :], SSD_NEG))
                ms.append((cb * (lf * dtt[hd:hd + 1, :] + lb * dtt[hb_:hb_ + 1, :])).astype(BF16))
                xs.append(jnp.where(glane == r, xg, jnp.zeros_like(xg)))
            y = jnp.dot(jnp.concatenate(ms, axis=1), jnp.concatenate(xs, axis=0), preferred_element_type=F32)
            y = y + jnp.dot(cg, hf_ref[g].astype(BF16), preferred_element_type=F32) * ef[:, gl]
            y = y + jnp.dot(cg, hbs_ref[cc, g], preferred_element_type=F32) * eb[:, gl]
            xgf = xg.astype(F32)
            y = y + dskip_ref[:, gl] * xgf
            yz = y * _silu(z_ref[:, gl].astype(F32))
            yn = yz * lax.rsqrt(jnp.mean(yz * yz, axis=-1, keepdims=True) + EPS) * normw_ref[:, gl]
            o_ref[:, gl] = yn.astype(BF16)
            hf_ref[g] = hf_ref[g] * dec[:, gl] + _group_state(bg, xg, wf[:, gl])


def ssd_scan(xbc, z, dt, alog128, dskip_row, norm_w, e_all, h0f, h0b, n_batch, seq):
    nc = seq // SSD_CHUNK
    q = SSD_CHUNK
    chunk = lambda b, s: b * nc + jnp.where(s < nc, nc - 1 - s, s - nc)
    chunk_b = lambda b, s: b * nc + jnp.maximum(s - nc, 0)
    st_spec = pl.BlockSpec((None, SSD_GROUPS, D_STATE, GHP), lambda b, s: (b, 0, 0, 0))
    row = lambda n: pl.BlockSpec((1, n), lambda b, s: (0, 0))
    return pl.pallas_call(
        functools.partial(_ssd_scan_kernel, nc=nc),
        grid=(n_batch, 2 * nc),
        in_specs=[
            pl.BlockSpec((q, HP), lambda b, s: (chunk(b, s), 0)),
            pl.BlockSpec((q, SSD_GN), lambda b, s: (chunk(b, s), HP // SSD_GN)),
            pl.BlockSpec((q, SSD_GN), lambda b, s: (chunk_b(b, s), HP // SSD_GN + 1)),
            pl.BlockSpec((q, HP), lambda b, s: (chunk_b(b, s), 0)),
            pl.BlockSpec((q, 128), lambda b, s: (chunk(b, s), 0)),
            row(128), row(HP), row(HP),
            pl.BlockSpec((4, 256, HP), lambda b, s: (0, 0, 0), pipeline_mode=pl.Buffered(1)),
            st_spec, st_spec,
        ],
        out_specs=pl.BlockSpec((q, HP), lambda b, s: (chunk_b(b, s), 0)),
        out_shape=jax.ShapeDtypeStruct((n_batch * seq, HP), BF16),
        scratch_shapes=[
            pltpu.VMEM((SSD_GROUPS, D_STATE, GHP), F32),
            pltpu.VMEM((SSD_GROUPS, D_STATE, GHP), F32),
            pltpu.VMEM((nc, SSD_GROUPS, D_STATE, GHP), BF16),
        ],
        compiler_params=pltpu.CompilerParams(
            dimension_semantics=("arbitrary", "arbitrary"), vmem_limit_bytes=VMEM_LIMIT_BYTES),
        name="ssd_scan",
    )(xbc, xbc, xbc, z, dt, alog128, dskip_row, norm_w.reshape(1, HP), e_all, h0f, h0b)


def _ssd_expand_matrices():
    hd = jnp.arange(HP) // SSD_HEADDIM
    rows = jnp.arange(256)
    mats = []
    for k in range(4):
        sel = ((rows % 128) // SSD_HEADS == k)
        mats.append(((rows % SSD_HEADS)[:, None] == hd[None, :]) & sel[:, None])
    return jnp.stack(mats).astype(BF16)


def ssd_layer(xs, n_lat, n_ctx, seq, clen, n_batch, mods, nw, in_w, conv_w, conv_b, dt_bias, a_log, d_skip,
              norm_w, out_w, debug=False):
    w_z = in_w[:, :D_INNER].astype(BF16)
    w_dt2 = in_w[:, D_INNER:D_INNER + 2 * SSD_HEADS]
    w_dt = jnp.concatenate([w_dt2, w_dt2], axis=1).astype(BF16)
    w_xbc = in_w[:, D_INNER + 2 * SSD_HEADS:].astype(BF16)
    both = lambda t: jnp.concatenate([t.reshape(1, 2 * SSD_HEADS)] * 2, axis=1)
    dt_bias128, alog128 = both(dt_bias), both(a_log)
    dskip_row = jnp.repeat(d_skip, SSD_HEADDIM).reshape(1, HP)
    e_all = _ssd_expand_matrices()
    proj_args = (mods, nw, w_z, w_dt, w_xbc, conv_w, conv_b, dt_bias128)

    ctm = min(clen, FFN_TM)
    _, dtc, xbcc = ssd_proj(xs, n_lat, n_ctx, clen, ctm, proj_args[0], lambda i: n_batch, *proj_args[1:])
    h0f, h0b = ssd_context_states(xbcc, dtc, alog128, e_all, n_batch, clen)
    tpb = seq // FFN_TM
    zl, dtl, xbcl = ssd_proj(xs, 0, n_lat, seq, FFN_TM, proj_args[0], lambda i: i // tpb, *proj_args[1:])
    yn = ssd_scan(xbcl, zl, dtl, alog128, dskip_row, norm_w, e_all, h0f, h0b, n_batch, seq)
    out = proj_residual(yn, xs, mods, 5, out_w.astype(BF16), jnp.zeros((D_MODEL,), F32), seq, n_batch)
    if debug:
        return out, (h0f, h0b, zl, dtl, xbcl)
    return out


def _rmsnorm(x, w):
    return x * lax.rsqrt(jnp.mean(x * x, axis=-1, keepdims=True) + EPS) * w


def _layernorm(x, w, b):
    mu = jnp.mean(x, axis=-1, keepdims=True)
    var = jnp.mean(jnp.square(x - mu), axis=-1, keepdims=True)
    return (x - mu) * lax.rsqrt(var + EPS) * w + b


def _grouped_rmsnorm(y, w):
    shp = y.shape
    yf = y.reshape(shp[:-1] + (SSD_GROUPS, D_INNER // SSD_GROUPS))
    yf = yf * lax.rsqrt(jnp.mean(yf * yf, axis=-1, keepdims=True) + EPS)
    return yf.reshape(shp) * w


def _dwconv1d(u, k, b):
    y = lax.conv_general_dilated(u, k[:, None, :], window_strides=(1,), padding='SAME',
                                 dimension_numbers=('NWC', 'WIO', 'NWC'), feature_group_count=u.shape[-1])
    return y + b


def _dwconv2d(u4, k4):
    return lax.conv_general_dilated(u4, k4, window_strides=(1, 1), padding='SAME',
                                    dimension_numbers=('NHWC', 'HWIO', 'NHWC'), feature_group_count=u4.shape[-1])


def _conv_module(h, pw1_w, pw1_b, dw_w, dw_b, ln_w, ln_b, pw2_w, pw2_b, rows):
    a, g = jnp.split(h @ pw1_w + pw1_b, 2, axis=-1)
    u = a * jax.nn.sigmoid(g)
    if rows is None:
        u = _dwconv1d(u, dw_w, dw_b)
    else:
        bsz, seq, d = u.shape
        half = d // 2
        u4 = u.reshape(bsz, rows, GRID_W, d)
        y_h = _dwconv2d(u4[..., :half], dw_w[:, :half][None, :, None, :])
        y_v = _dwconv2d(u4[..., half:], dw_w[:, half:][:, None, None, :])
        u = jnp.concatenate([y_h, y_v], axis=-1).reshape(bsz, seq, d) + dw_b
    u = jax.nn.silu(_layernorm(u, ln_w, ln_b))
    return u @ pw2_w + pw2_b


def _ssd_chunks(xh, dt_raw, bm, dt_bias, a_log):
    bsz, seq = xh.shape[:2]
    nc = seq // SSD_CHUNK
    shp = (bsz, nc, SSD_CHUNK, SSD_GROUPS, SSD_HPG)
    dt = jax.nn.softplus(dt_raw + dt_bias).reshape(shp)
    a = -jnp.exp(a_log).reshape(SSD_GROUPS, SSD_HPG)
    acs = jnp.cumsum(dt * a, axis=2)
    xg = xh.reshape(shp + (SSD_HEADDIM,))
    bg = bm.reshape(bsz, nc, SSD_CHUNK, SSD_GROUPS, D_STATE)
    return xg, dt, acs, bg


def _ssd_chunk_states(xg, dt, acs, bg, h0, collect):
    decay_end = jnp.exp(acs[:, :, -1:] - acs)
    st = jnp.einsum('bcqgn,bcqgr,bcqgrp->bcgrpn', bg, decay_end * dt, xg)
    chunk_decay = jnp.exp(acs[:, :, -1])

    def step(hc, inp):
        s_c, d_c = inp
        return d_c[..., None, None] * hc + s_c, (hc if collect else None)

    h_final, h_starts = lax.scan(step, h0, (jnp.moveaxis(st, 1, 0), jnp.moveaxis(chunk_decay, 1, 0)))
    return (jnp.moveaxis(h_starts, 0, 1) if collect else None), h_final


def _ssd_scan(xh, dt_raw, bm, cm, dt_bias, a_log, h0):
    bsz, seq = xh.shape[:2]
    xg, dt, acs, bg = _ssd_chunks(xh, dt_raw, bm, dt_bias, a_log)
    cg = cm.reshape(bg.shape)
    h_starts, h_final = _ssd_chunk_states(xg, dt, acs, bg, h0, True)
    cb = jnp.einsum('bcign,bcjgn->bcgij', cg, bg)
    acs_t = jnp.moveaxis(acs, 2, -1)
    diff = acs_t[..., :, None] - acs_t[..., None, :]
    in_order = jnp.tril(jnp.ones((SSD_CHUNK, SSD_CHUNK), dtype=bool))
    lmat = jnp.exp(jnp.where(in_order, diff, -jnp.inf))
    y_diag = jnp.einsum('bcgij,bcgrij,bcjgr,bcjgrp->bcigrp', cb, lmat, dt, xg)
    y_off = jnp.einsum('bcign,bcgrpn,bcigr->bcigrp', cg, h_starts, jnp.exp(acs))
    y = (y_diag + y_off).reshape(bsz, seq, SSD_HEADS, SSD_HEADDIM)
    return y, h_final


def _flip(t):
    return jnp.flip(t, axis=1)


def _ssd_bidir(h, in_w, conv_w, conv_b, dt_bias, a_log, d_skip, norm_w, out_w, h0_f, h0_b):
    bsz, seq = h.shape[:2]
    proj = h @ in_w
    z = proj[..., :D_INNER]
    dt_raw = proj[..., D_INNER:D_INNER + 2 * SSD_HEADS]
    xbc = jax.nn.silu(_dwconv1d(proj[..., D_INNER + 2 * SSD_HEADS:], conv_w, conv_b))
    xh = xbc[..., :D_INNER].reshape(bsz, seq, SSD_HEADS, SSD_HEADDIM)
    bm = xbc[..., D_INNER:D_INNER + SSD_GN]
    cm = xbc[..., D_INNER + SSD_GN:]
    y_f, h_f = _ssd_scan(xh, dt_raw[..., :SSD_HEADS], bm, cm, dt_bias[0], a_log[0], h0_f)
    y_b, h_b = _ssd_scan(_flip(xh), _flip(dt_raw[..., SSD_HEADS:]), _flip(bm), _flip(cm),
                         dt_bias[1], a_log[1], h0_b)
    y = y_f + _flip(y_b) + d_skip[:, None] * xh
    y = y.reshape(bsz, seq, D_INNER)
    y = _grouped_rmsnorm(y * jax.nn.silu(z), norm_w)
    return y @ out_w, h_f, h_b


def _ssd_context_states(hc, in_w, conv_w, conv_b, dt_bias, a_log):
    bsz, clen = hc.shape[:2]
    proj = hc @ in_w[:, D_INNER:D_INNER + 2 * SSD_HEADS + D_INNER + SSD_GN]
    dt_raw = proj[..., :2 * SSD_HEADS]
    xb = jax.nn.silu(_dwconv1d(proj[..., 2 * SSD_HEADS:], conv_w[:, :D_INNER + SSD_GN],
                               conv_b[:D_INNER + SSD_GN]))
    xh = xb[..., :D_INNER].reshape(bsz, clen, SSD_HEADS, SSD_HEADDIM)
    bm = xb[..., D_INNER:]
    h0 = jnp.zeros((bsz, SSD_GROUPS, SSD_HPG, SSD_HEADDIM, D_STATE), jnp.float32)
    xg, dt, acs, bg = _ssd_chunks(xh, dt_raw[..., :SSD_HEADS], bm, dt_bias[0], a_log[0])
    _, h_f = _ssd_chunk_states(xg, dt, acs, bg, h0, False)
    xg, dt, acs, bg = _ssd_chunks(_flip(xh), _flip(dt_raw[..., SSD_HEADS:]), _flip(bm), dt_bias[1], a_log[1])
    _, h_b = _ssd_chunk_states(xg, dt, acs, bg, h0, False)
    return h_f, h_b


def _chunk_mlp(h, in_w, in_b, ln_w, ln_b, sp_w, sp_b, out_w, out_b):
    bsz, seq, _ = h.shape
    z = jax.nn.gelu(h @ in_w + in_b, approximate=False)
    u, v = jnp.split(z, 2, axis=-1)
    v = _layernorm(v, ln_w, ln_b).reshape(bsz, seq // CHUNK, CHUNK, CMLP_GROUPS, CMLP_GD)
    v = jnp.einsum('gij,bcjgd->bcigd', sp_w, v) + sp_b.T[:, :, None]
    return (u * v.reshape(bsz, seq, CMLP_E)) @ out_w + out_b


def kernel(x, c, ctx, c_ctx, ada_w, ada_b, norm_w, ffn_w_in, ffn_w_out, conv_pw1_w, conv_pw1_b, conv_dw_w, conv_dw_b, conv_ln_w, conv_ln_b, conv_pw2_w, conv_pw2_b, ssd_in_w, ssd_conv_w, ssd_conv_b, ssd_dt_bias, ssd_a_log, ssd_d, ssd_norm_w, ssd_out_w, cmlp_in_w, cmlp_in_b, cmlp_ln_w, cmlp_ln_b, cmlp_sp_w, cmlp_sp_b, cmlp_out_w, cmlp_out_b, final_norm_w):
    bsz, seq, d = x.shape
    clen = ctx.shape[1]
    rows = seq // GRID_W
    n_lat = bsz * seq
    n_all = n_lat + bsz * clen
    last_ssd = max((i for i in range(DEPTH) if i % N_MIXERS == 1), default=-1)

    cvec = jnp.concatenate([c, c_ctx[None, :], jnp.zeros((MOD_ROWS - bsz - 1, d), F32)], axis=0)
    mods_all = ada_modulation(cvec, ada_w, ada_b)
    w_in_b = ffn_w_in.astype(BF16)
    w_out_b = ffn_w_out.astype(BF16)

    xs = jnp.concatenate([x.reshape(n_lat, d), ctx.reshape(bsz * clen, d)], axis=0)
    for i in range(DEPTH):
        kind, j = i % N_MIXERS, i // N_MIXERS
        ctx_live = i <= last_ssd
        ctx_carry = i < last_ssd
        mods = mods_all[i].reshape(MOD_ROWS * N_MODS, 1, d)
        ml = [mods_all[i, :bsz, m * d:(m + 1) * d][:, None, :] for m in range(N_MODS)]
        mc = [mods_all[i, bsz, m * d:(m + 1) * d] for m in range(N_MODS)]

        xs = ffn_block(xs, n_all if ctx_live else n_lat, mods, 0, norm_w[i, 0], w_in_b[i, 0], w_out_b[i, 0],
                       seq, bsz)

        if kind == 1:
            assert ctx_live and not ctx_carry, "context is only carried up to the last SSD layer"
            xs = ssd_layer(xs, n_lat, n_all - n_lat, seq, clen, bsz, mods, norm_w[i, 1], ssd_in_w[j], ssd_conv_w[j],
                           ssd_conv_b[j], ssd_dt_bias[j], ssd_a_log[j], ssd_d[j], ssd_norm_w[j], ssd_out_w[j])
            xs = ffn_block(xs, n_lat, mods, 6, norm_w[i, 2], w_in_b[i, 1], w_out_b[i, 1], seq, bsz)
            continue

        xl = xs[:n_lat].reshape(bsz, seq, d)
        h = _rmsnorm(xl, norm_w[i, 1]) * (1.0 + ml[4]) + ml[3]
        if ctx_live:
            xc = xs[n_lat:].reshape(bsz, clen, d)
            hc = _rmsnorm(xc, norm_w[i, 1]) * (1.0 + mc[4]) + mc[3]
        y_c = None
        if kind == 0:
            p = (conv_pw1_w[j], conv_pw1_b[j], conv_dw_w[j], conv_dw_b[j],
                 conv_ln_w[j], conv_ln_b[j], conv_pw2_w[j], conv_pw2_b[j])
            y = _conv_module(h, *p, rows)
            if ctx_carry:
                y_c = _conv_module(hc, *p, None)
        elif kind == 1:
            p = (ssd_in_w[j], ssd_conv_w[j], ssd_conv_b[j], ssd_dt_bias[j], ssd_a_log[j],
                 ssd_d[j], ssd_norm_w[j], ssd_out_w[j])
            if ctx_carry:
                h0 = jnp.zeros((bsz, SSD_GROUPS, SSD_HPG, SSD_HEADDIM, D_STATE), jnp.float32)
                y_c, hc_f, hc_b = _ssd_bidir(hc, *p, h0, h0)
            else:
                hc_f, hc_b = _ssd_context_states(hc, ssd_in_w[j], ssd_conv_w[j], ssd_conv_b[j],
                                                 ssd_dt_bias[j], ssd_a_log[j])
            y, _, _ = _ssd_bidir(h, *p, hc_f, hc_b)
        else:
            p = (cmlp_in_w[j], cmlp_in_b[j], cmlp_ln_w[j], cmlp_ln_b[j],
                 cmlp_sp_w[j], cmlp_sp_b[j], cmlp_out_w[j], cmlp_out_b[j])
            y = _chunk_mlp(h, *p)
            if ctx_carry:
                y_c = _chunk_mlp(hc, *p)
        xl = xl + ml[5] * y
        if ctx_carry:
            xc = xc + mc[5] * y_c
            xs = jnp.concatenate([xl.reshape(n_lat, d), xc.reshape(bsz * clen, d)], axis=0)
        else:
            xs = xl.reshape(n_lat, d)

        xs = ffn_block(xs, n_all if ctx_carry else n_lat, mods, 6, norm_w[i, 2], w_in_b[i, 1], w_out_b[i, 1],
                       seq, bsz)
    out = _rmsnorm(xs, final_norm_w)
    return out.reshape(bsz, seq, d)
```

```python
import functools
import math

import jax
import jax.numpy as jnp
from jax import lax
from jax.experimental import pallas as pl
from jax.experimental.pallas import tpu as pltpu

D_MODEL = 1024
DEPTH = 4
GRID_W = 64
N_MIXERS = 3
N_MODS = 9
EPS = 1e-6
D_FF = 2816
CONV_WIDTH = 31
D_INNER = 2 * D_MODEL
SSD_HEADDIM = 64
SSD_HEADS = D_INNER // SSD_HEADDIM
SSD_GROUPS = 8
SSD_HPG = SSD_HEADS // SSD_GROUPS
D_STATE = 128
SSD_CONV = 7
SSD_CHUNK = 128
SSD_GN = SSD_GROUPS * D_STATE
SSD_CONV_DIM = D_INNER + 2 * SSD_GN
CHUNK = 128
CMLP_E = 2 * D_MODEL
CMLP_GROUPS = 8
CMLP_GD = CMLP_E // CMLP_GROUPS

SUBLANES = 8
LANES = 128
VMEM_LIMIT_BYTES = 56 * 1024 * 1024

MOD_ROWS = 24
FFN_TM = 512
FFN_CHUNK = 256
BF16 = jnp.bfloat16
F32 = jnp.float32


def _silu(v):
    return v * jax.nn.sigmoid(v)


def _ada_kernel(c_ref, w_ref, b_ref, o_ref):
    s = _silu(c_ref[...]).astype(BF16)
    o_ref[...] = jnp.dot(s, w_ref[...].astype(BF16), preferred_element_type=F32) + b_ref[...]


def ada_modulation(cvec, ada_w, ada_b):
    depth, d, n = ada_w.shape
    tn = d
    return pl.pallas_call(
        _ada_kernel,
        grid=(depth, n // tn),
        in_specs=[
            pl.BlockSpec((MOD_ROWS, d), lambda l, j: (0, 0)),
            pl.BlockSpec((None, d, tn), lambda l, j: (l, 0, j)),
            pl.BlockSpec((None, 1, tn), lambda l, j: (l, 0, j)),
        ],
        out_specs=pl.BlockSpec((None, MOD_ROWS, tn), lambda l, j: (l, 0, j)),
        out_shape=jax.ShapeDtypeStruct((depth, MOD_ROWS, n), F32),
        compiler_params=pltpu.CompilerParams(
            dimension_semantics=("arbitrary", "arbitrary"), vmem_limit_bytes=VMEM_LIMIT_BYTES),
        name="ada_modulation",
    )(cvec, ada_w, ada_b.reshape(depth, 1, n))


def _ffn_kernel(x_ref, shift_ref, scale_ref, gate_ref, nw_ref, win_ref, wout_ref, *rest):
    fw_ref, o_ref, a_ref = rest if len(rest) == 3 else (None,) + rest
    x = x_ref[...]
    xn = x * lax.rsqrt(jnp.mean(x * x, axis=-1, keepdims=True) + EPS) * nw_ref[...]
    hb = (xn * (1.0 + scale_ref[...]) + shift_ref[...]).astype(BF16)
    for j in range(D_FF // FFN_CHUNK):
        lo = j * FFN_CHUNK
        g = jnp.dot(hb, win_ref[:, lo:lo + FFN_CHUNK], preferred_element_type=F32)
        u = jnp.dot(hb, win_ref[:, D_FF + lo:D_FF + lo + FFN_CHUNK], preferred_element_type=F32)
        a_ref[:, lo:lo + FFN_CHUNK] = (_silu(g) * u).astype(BF16)
    y = jnp.dot(a_ref[...], wout_ref[...], preferred_element_type=F32)
    out = x + (0.5 * gate_ref[...]) * y
    if fw_ref is not None:
        out = out * lax.rsqrt(jnp.mean(out * out, axis=-1, keepdims=True) + EPS) * fw_ref[...]
    o_ref[...] = out


def _mod_spec(m, tiles_per_batch, n_batch):
    return pl.BlockSpec(
        (None, 1, D_MODEL),
        lambda i: (jnp.minimum(i // tiles_per_batch, n_batch) * N_MODS + m, 0, 0))


def ffn_block(xs, n_tokens, mods, m0, nw, w_in, w_out, seq, n_batch, final_w=None):
    d = xs.shape[-1]
    tm = FFN_TM
    tpb = seq // tm
    const = lambda i: (0, 0)
    final = [] if final_w is None else [final_w.reshape(1, d)]
    return pl.pallas_call(
        _ffn_kernel,
        grid=(n_tokens // tm,),
        in_specs=[
            pl.BlockSpec((tm, d), lambda i: (i, 0)),
            _mod_spec(m0, tpb, n_batch), _mod_spec(m0 + 1, tpb, n_batch), _mod_spec(m0 + 2, tpb, n_batch),
            pl.BlockSpec((1, d), const),
            pl.BlockSpec((d, 2 * D_FF), const, pipeline_mode=pl.Buffered(1)),
            pl.BlockSpec((D_FF, d), const, pipeline_mode=pl.Buffered(1)),
        ] + [pl.BlockSpec((1, d), const)] * len(final),
        out_specs=pl.BlockSpec((tm, d), lambda i: (i, 0)),
        out_shape=jax.ShapeDtypeStruct((n_tokens, d), F32),
        scratch_shapes=[pltpu.VMEM((tm, D_FF), BF16)],
        compiler_params=pltpu.CompilerParams(
            dimension_semantics=("arbitrary",), vmem_limit_bytes=VMEM_LIMIT_BYTES),
        name="ffn_block",
    )(xs, mods, mods, mods, nw.reshape(1, d), w_in, w_out, *final)


def _proj_res_kernel(a_ref, x_ref, gate_ref, w_ref, b_ref, o_ref):
    y = jnp.dot(a_ref[...], w_ref[...], preferred_element_type=F32) + b_ref[...]
    o_ref[...] = x_ref[...] + gate_ref[...] * y


def proj_residual(a, xs, mods, m_gate, w, bias, seq, n_batch):
    n_tok, k = a.shape
    d = w.shape[1]
    tm = FFN_TM
    tpb = seq // tm
    const = lambda i: (0, 0)
    return pl.pallas_call(
        _proj_res_kernel,
        grid=(n_tok // tm,),
        in_specs=[
            pl.BlockSpec((tm, k), lambda i: (i, 0)),
            pl.BlockSpec((tm, d), lambda i: (i, 0)),
            _mod_spec(m_gate, tpb, n_batch),
            pl.BlockSpec((k, d), const, pipeline_mode=pl.Buffered(1)),
            pl.BlockSpec((1, d), const),
        ],
        out_specs=pl.BlockSpec((tm, d), lambda i: (i, 0)),
        out_shape=jax.ShapeDtypeStruct((n_tok, d), F32),
        compiler_params=pltpu.CompilerParams(
            dimension_semantics=("arbitrary",), vmem_limit_bytes=VMEM_LIMIT_BYTES),
        name="proj_residual",
    )(a, xs, mods, w, bias.reshape(1, d))


SSD_HALO = SUBLANES
SSD_NEG = -1e30
SSD_XBC_CHUNK = 512
HP = D_INNER
GHP = D_INNER // SSD_GROUPS


def _softplus(v):
    return jnp.maximum(v, 0.0) + jnp.log1p(jnp.exp(-jnp.abs(v)))


def _modnorm(x, nw, shift, scale):
    xn = x * lax.rsqrt(jnp.mean(x * x, axis=-1, keepdims=True) + EPS) * nw
    return xn * (1.0 + scale) + shift


def _ssd_proj_kernel(xp_ref, x_ref, xn_ref, shift_ref, scale_ref, nw_ref, wz_ref, wdt_ref, wxbc_ref,
                     cw_ref, cb_ref, dtb_ref, z_ref, dt_ref, xbc_ref, ext_ref, *, tiles_per_seq, tm):
    pos = pl.program_id(0) % tiles_per_seq
    nw, shift, scale = nw_ref[...], shift_ref[...], scale_ref[...]
    h = _modnorm(x_ref[...], nw, shift, scale)
    hp = _modnorm(xp_ref[...], nw, shift, scale) * (pos > 0).astype(F32)
    hn = _modnorm(xn_ref[...], nw, shift, scale) * (pos < tiles_per_seq - 1).astype(F32)
    hb = h.astype(BF16)
    hext = jnp.concatenate([hp, h, hn], axis=0).astype(BF16)

    for j in range(D_INNER // SSD_XBC_CHUNK):
        cols = slice(j * SSD_XBC_CHUNK, (j + 1) * SSD_XBC_CHUNK)
        z_ref[:, cols] = jnp.dot(hb, wz_ref[:, cols], preferred_element_type=F32).astype(BF16)
    dt_ref[...] = _softplus(jnp.dot(hb, wdt_ref[...], preferred_element_type=F32) + dtb_ref[...])

    off = SSD_HALO - SSD_CONV // 2
    slabs_per_chunk = SSD_XBC_CHUNK // LANES
    for j in range(SSD_CONV_DIM // SSD_XBC_CHUNK):
        cols = slice(j * SSD_XBC_CHUNK, (j + 1) * SSD_XBC_CHUNK)
        e = jnp.dot(hext, wxbc_ref[:, cols], preferred_element_type=F32)
        for s in range(slabs_per_chunk):
            sl = j * slabs_per_chunk + s
            lc = slice(sl * LANES, (sl + 1) * LANES)
            ext_ref[sl] = e[:, s * LANES:(s + 1) * LANES]
            acc = ext_ref[sl, pl.ds(off, tm, stride=1), :] * cw_ref[0:1, lc] + cb_ref[:, lc]
            for k in range(1, SSD_CONV):
                acc = acc + ext_ref[sl, pl.ds(off + k, tm, stride=1), :] * cw_ref[k:k + 1, lc]
            xbc_ref[:, lc] = _silu(acc).astype(BF16)


def ssd_proj(xs, tok0, n_tok, seq, tm, mods, mod_row, nw, w_z, w_dt, w_xbc, conv_w, conv_b, dt_bias128):
    d = xs.shape[-1]
    tps = seq // tm
    t0 = tok0 // tm
    hb = tm // SSD_HALO
    n_hblk = xs.shape[0] // SSD_HALO
    const = lambda i: (0, 0)
    mspec = lambda m: pl.BlockSpec((None, 1, d), lambda i: (mod_row(i) * N_MODS + m, 0, 0))
    return pl.pallas_call(
        functools.partial(_ssd_proj_kernel, tiles_per_seq=tps, tm=tm),
        grid=(n_tok // tm,),
        in_specs=[
            pl.BlockSpec((SSD_HALO, d), lambda i: (jnp.maximum((i + t0) * hb - 1, 0), 0)),
            pl.BlockSpec((tm, d), lambda i: (i + t0, 0)),
            pl.BlockSpec((SSD_HALO, d), lambda i: (jnp.minimum((i + t0 + 1) * hb, n_hblk - 1), 0)),
            mspec(3), mspec(4),
            pl.BlockSpec((1, d), const),
            pl.BlockSpec((d, D_INNER), const, pipeline_mode=pl.Buffered(1)),
            pl.BlockSpec((d, 128), const, pipeline_mode=pl.Buffered(1)),
            pl.BlockSpec((d, SSD_CONV_DIM), const, pipeline_mode=pl.Buffered(1)),
            pl.BlockSpec((SSD_CONV + 1, SSD_CONV_DIM), const),
            pl.BlockSpec((1, SSD_CONV_DIM), const),
            pl.BlockSpec((1, 128), const),
        ],
        out_specs=[
            pl.BlockSpec((tm, D_INNER), lambda i: (i, 0)),
            pl.BlockSpec((tm, 128), lambda i: (i, 0)),
            pl.BlockSpec((tm, SSD_CONV_DIM), lambda i: (i, 0)),
        ],
        out_shape=[
            jax.ShapeDtypeStruct((n_tok, D_INNER), BF16),
            jax.ShapeDtypeStruct((n_tok, 128), F32),
            jax.ShapeDtypeStruct((n_tok, SSD_CONV_DIM), BF16),
        ],
        scratch_shapes=[pltpu.VMEM((SSD_CONV_DIM // LANES, tm + 2 * SSD_HALO, LANES), F32)],
        compiler_params=pltpu.CompilerParams(
            dimension_semantics=("arbitrary",), vmem_limit_bytes=VMEM_LIMIT_BYTES),
        name="ssd_proj",
    )(xs, xs, xs, mods, mods, nw.reshape(1, d), w_z, w_dt, w_xbc,
      jnp.concatenate([conv_w, jnp.zeros((1, SSD_CONV_DIM), F32)], axis=0), conv_b.reshape(1, SSD_CONV_DIM),
      dt_bias128)


def _split3(v):
    hi = v.astype(BF16)
    r = v - hi.astype(F32)
    mid = r.astype(BF16)
    lo = (r - mid.astype(F32)).astype(BF16)
    return hi, mid, lo


def _dot3_left(tri, v):
    hi, mid, lo = _split3(v)
    return (jnp.dot(tri, hi, preferred_element_type=F32) + jnp.dot(tri, mid, preferred_element_type=F32)
            + jnp.dot(tri, lo, preferred_element_type=F32))


def _dot3_right(v, tri):
    hi, mid, lo = _split3(v)
    return (jnp.dot(hi, tri, preferred_element_type=F32) + jnp.dot(mid, tri, preferred_element_type=F32)
            + jnp.dot(lo, tri, preferred_element_type=F32))


def _chunk_decays(dt128, alog128, with_rows):
    q = dt128.shape[0]
    ri = lax.broadcasted_iota(jnp.int32, (q, q), 0)
    ci = lax.broadcasted_iota(jnp.int32, (q, q), 1)
    ltri = (ci <= ri).astype(BF16)
    utri = (ci >= ri).astype(BF16)
    lane = lax.broadcasted_iota(jnp.int32, (1, 128), 1)
    lane_fwd = (lane % 64) < 32
    v = dt128 * (-jnp.exp(alog128))
    csl = _dot3_left(ltri, v)
    csu = _dot3_left(utri, v)
    cs = jnp.where(lane_fwd, csl, csu)
    edge = jnp.where(lane_fwd, csl[q - 1:q, :], csu[0:1, :])
    qm = jnp.where(lane < 64, jnp.exp(cs), jnp.exp(edge - cs) * dt128)
    if not with_rows:
        return cs, edge, qm, None, None
    vt = v.T
    row = lax.broadcasted_iota(jnp.int32, (128, 1), 0)
    row_fwd = (row % 64) < 32
    cst = jnp.where(row_fwd, _dot3_right(vt, utri), _dot3_right(vt, ltri))
    return cs, edge, qm, cst, dt128.T


def _expand_heads(qm, e_mat):
    hi = qm.astype(BF16)
    lo = (qm - hi.astype(F32)).astype(BF16)
    return jnp.dot(jnp.concatenate([hi, lo], axis=1), e_mat, preferred_element_type=F32)


def _group_state(bg, xg, wg):
    xw = (xg.astype(F32) * wg).astype(BF16)
    return lax.dot_general(bg, xw, (((0,), (0,)), ((), ())), preferred_element_type=F32)


def _ssd_states_kernel(xf_ref, bf_ref, dtf_ref, xb_ref, bb_ref, dtb_ref, alog_ref, e_ref, hf_ref, hb_ref):
    s = pl.program_id(1)

    @pl.when(s == 0)
    def _():
        hf_ref[...] = jnp.zeros_like(hf_ref)
        hb_ref[...] = jnp.zeros_like(hb_ref)

    for x_ref, b_ref, dt_ref, st_ref, k_w, k_dec, edge_row in (
            (xf_ref, bf_ref, dtf_ref, hf_ref, 2, 0, SSD_CHUNK - 1), (xb_ref, bb_ref, dtb_ref, hb_ref, 3, 1, 0)):
        _, _, qm, _, _ = _chunk_decays(dt_ref[...], alog_ref[...], False)
        w = _expand_heads(qm, e_ref[k_w])
        dec = _expand_heads(qm, e_ref[k_dec])[edge_row:edge_row + 1, :]
        x = x_ref[...]
        bm = b_ref[...]
        for g in range(SSD_GROUPS):
            gl = slice(g * GHP, (g + 1) * GHP)
            st = _group_state(bm[:, g * D_STATE:(g + 1) * D_STATE], x[:, gl], w[:, gl])
            st_ref[g] = st_ref[g] * dec[:, gl] + st


def ssd_context_states(xbc, dt, alog128, e_all, n_batch, seq):
    nc = seq // SSD_CHUNK
    q = SSD_CHUNK
    st_shape = jax.ShapeDtypeStruct((n_batch, SSD_GROUPS, D_STATE, GHP), F32)
    st_spec = pl.BlockSpec((None, SSD_GROUPS, D_STATE, GHP), lambda b, s: (b, 0, 0, 0))
    fwd = lambda b, s: b * nc + s
    bwd = lambda b, s: b * nc + nc - 1 - s
    return pl.pallas_call(
        _ssd_states_kernel,
        grid=(n_batch, nc),
        in_specs=[
            pl.BlockSpec((q, HP), lambda b, s: (fwd(b, s), 0)),
            pl.BlockSpec((q, SSD_GN), lambda b, s: (fwd(b, s), HP // SSD_GN)),
            pl.BlockSpec((q, 128), lambda b, s: (fwd(b, s), 0)),
            pl.BlockSpec((q, HP), lambda b, s: (bwd(b, s), 0)),
            pl.BlockSpec((q, SSD_GN), lambda b, s: (bwd(b, s), HP // SSD_GN)),
            pl.BlockSpec((q, 128), lambda b, s: (bwd(b, s), 0)),
            pl.BlockSpec((1, 128), lambda b, s: (0, 0)),
            pl.BlockSpec((4, 256, HP), lambda b, s: (0, 0, 0), pipeline_mode=pl.Buffered(1)),
        ],
        out_specs=[st_spec, st_spec],
        out_shape=[st_shape, st_shape],
        compiler_params=pltpu.CompilerParams(
            dimension_semantics=("arbitrary", "arbitrary"), vmem_limit_bytes=VMEM_LIMIT_BYTES),
        name="ssd_context_states",
    )(xbc, xbc, dt, xbc, xbc, dt, alog128, e_all)


def _ssd_scan_kernel(x_ref, b_ref, c_ref, z_ref, dt_ref, alog_ref, dskip_ref, normw_ref, e_ref, h0f_ref, h0b_ref,
                     o_ref, hf_ref, hb_ref, hbs_ref, *, nc):
    s = pl.program_id(1)
    q = SSD_CHUNK

    @pl.when(s == 0)
    def _():
        hf_ref[...] = h0f_ref[...]
        hb_ref[...] = h0b_ref[...]

    @pl.when(s < nc)
    def _():
        cc = nc - 1 - s
        _, _, qm, _, _ = _chunk_decays(dt_ref[...], alog_ref[...], False)
        w = _expand_heads(qm, e_ref[3])
        dec = _expand_heads(qm, e_ref[1])[0:1, :]
        x = x_ref[...]
        bm = b_ref[...]
        for g in range(SSD_GROUPS):
            gl = slice(g * GHP, (g + 1) * GHP)
            hbs_ref[cc, g] = hb_ref[g].astype(BF16)
            st = _group_state(bm[:, g * D_STATE:(g + 1) * D_STATE], x[:, gl], w[:, gl])
            hb_ref[g] = hb_ref[g] * dec[:, gl] + st

    @pl.when(s >= nc)
    def _():
        cc = s - nc
        cs, _, qm, cst, dtt = _chunk_decays(dt_ref[...], alog_ref[...], True)
        ef = _expand_heads(qm, e_ref[0])
        eb = _expand_heads(qm, e_ref[1])
        wf = _expand_heads(qm, e_ref[2])
        dec = ef[q - 1:q, :]
        x = x_ref[...]
        bm = b_ref[...]
        cm = c_ref[...]
        ri = lax.broadcasted_iota(jnp.int32, (q, q), 0)
        ci = lax.broadcasted_iota(jnp.int32, (q, q), 1)
        tril = ci <= ri
        triu = ci >= ri
        glane = lax.broadcasted_iota(jnp.int32, (1, GHP), 1) // SSD_HEADDIM
        for g in range(SSD_GROUPS):
            gl = slice(g * GHP, (g + 1) * GHP)
            bg = bm[:, g * D_STATE:(g + 1) * D_STATE]
            cg = cm[:, g * D_STATE:(g + 1) * D_STATE]
            xg = x[:, gl]
            cb = lax.dot_general(cg, bg, (((1,), (1,)), ((), ())), preferred_element_type=F32)
            ms, xs = [], []
            for r in range(SSD_HPG):
                hd = g * SSD_HPG + r
                hb_ = SSD_HEADS + hd
                lf = jnp.exp(jnp.where(tril, cs[:, hd:hd + 1] - cst[hd:hd + 1, :], SSD_NEG))
                lb = jnp.exp(jnp.where(triu, cs[:, hb_:hb_ + 1] - cst[hb_:hb_ + 1, :], SSD_NEG))
                ms.append((cb * (lf * dtt[hd:hd + 1, :] + lb * dtt[hb_:hb_ + 1, :])).astype(BF16))
                xs.append(jnp.where(glane == r, xg, jnp.zeros_like(xg)))
            y = jnp.dot(jnp.concatenate(ms, axis=1), jnp.concatenate(xs, axis=0), preferred_element_type=F32)
            y = y + jnp.dot(cg, hf_ref[g].astype(BF16), preferred_element_type=F32) * ef[:, gl]
            y = y + jnp.dot(cg, hbs_ref[cc, g], preferred_element_type=F32) * eb[:, gl]
            xgf = xg.astype(F32)
            y = y + dskip_ref[:, gl] * xgf
            yz = y * _silu(z_ref[:, gl].astype(F32))
            yn = yz * lax.rsqrt(jnp.mean(yz * yz, axis=-1, keepdims=True) + EPS) * normw_ref[:, gl]
            o_ref[:, gl] = yn.astype(BF16)
            hf_ref[g] = hf_ref[g] * dec[:, gl] + _group_state(bg, xg, wf[:, gl])


def ssd_scan(xbc, z, dt, alog128, dskip_row, norm_w, e_all, h0f, h0b, n_batch, seq):
    nc = seq // SSD_CHUNK
    q = SSD_CHUNK
    chunk = lambda b, s: b * nc + jnp.where(s < nc, nc - 1 - s, s - nc)
    chunk_b = lambda b, s: b * nc + jnp.maximum(s - nc, 0)
    st_spec = pl.BlockSpec((None, SSD_GROUPS, D_STATE, GHP), lambda b, s: (b, 0, 0, 0))
    row = lambda n: pl.BlockSpec((1, n), lambda b, s: (0, 0))
    return pl.pallas_call(
        functools.partial(_ssd_scan_kernel, nc=nc),
        grid=(n_batch, 2 * nc),
        in_specs=[
            pl.BlockSpec((q, HP), lambda b, s: (chunk(b, s), 0)),
            pl.BlockSpec((q, SSD_GN), lambda b, s: (chunk(b, s), HP // SSD_GN)),
            pl.BlockSpec((q, SSD_GN), lambda b, s: (chunk_b(b, s), HP // SSD_GN + 1)),
            pl.BlockSpec((q, HP), lambda b, s: (chunk_b(b, s), 0)),
            pl.BlockSpec((q, 128), lambda b, s: (chunk(b, s), 0)),
            row(128), row(HP), row(HP),
            pl.BlockSpec((4, 256, HP), lambda b, s: (0, 0, 0), pipeline_mode=pl.Buffered(1)),
            st_spec, st_spec,
        ],
        out_specs=pl.BlockSpec((q, HP), lambda b, s: (chunk_b(b, s), 0)),
        out_shape=jax.ShapeDtypeStruct((n_batch * seq, HP), BF16),
        scratch_shapes=[
            pltpu.VMEM((SSD_GROUPS, D_STATE, GHP), F32),
            pltpu.VMEM((SSD_GROUPS, D_STATE, GHP), F32),
            pltpu.VMEM((nc, SSD_GROUPS, D_STATE, GHP), BF16),
        ],
        compiler_params=pltpu.CompilerParams(
            dimension_semantics=("arbitrary", "arbitrary"), vmem_limit_bytes=VMEM_LIMIT_BYTES),
        name="ssd_scan",
    )(xbc, xbc, xbc, z, dt, alog128, dskip_row, norm_w.reshape(1, HP), e_all, h0f, h0b)


def _ssd_expand_matrices():
    hd = jnp.arange(HP) // SSD_HEADDIM
    rows = jnp.arange(256)
    mats = []
    for k in range(4):
        sel = ((rows % 128) // SSD_HEADS == k)
        mats.append(((rows % SSD_HEADS)[:, None] == hd[None, :]) & sel[:, None])
    return jnp.stack(mats).astype(BF16)


def ssd_layer(xs, n_lat, n_ctx, seq, clen, n_batch, mods, nw, in_w, conv_w, conv_b, dt_bias, a_log, d_skip,
              norm_w, out_w, debug=False):
    w_z = in_w[:, :D_INNER].astype(BF16)
    w_dt2 = in_w[:, D_INNER:D_INNER + 2 * SSD_HEADS]
    w_dt = jnp.concatenate([w_dt2, w_dt2], axis=1).astype(BF16)
    w_xbc = in_w[:, D_INNER + 2 * SSD_HEADS:].astype(BF16)
    both = lambda t: jnp.concatenate([t.reshape(1, 2 * SSD_HEADS)] * 2, axis=1)
    dt_bias128, alog128 = both(dt_bias), both(a_log)
    dskip_row = jnp.repeat(d_skip, SSD_HEADDIM).reshape(1, HP)
    e_all = _ssd_expand_matrices()
    proj_args = (mods, nw, w_z, w_dt, w_xbc, conv_w, conv_b, dt_bias128)

    ctm = min(clen, FFN_TM)
    _, dtc, xbcc = ssd_proj(xs, n_lat, n_ctx, clen, ctm, proj_args[0], lambda i: n_batch, *proj_args[1:])
    h0f, h0b = ssd_context_states(xbcc, dtc, alog128, e_all, n_batch, clen)
    tpb = seq // FFN_TM
    zl, dtl, xbcl = ssd_proj(xs, 0, n_lat, seq, FFN_TM, proj_args[0], lambda i: i // tpb, *proj_args[1:])
    yn = ssd_scan(xbcl, zl, dtl, alog128, dskip_row, norm_w, e_all, h0f, h0b, n_batch, seq)
    out = proj_residual(yn, xs, mods, 5, out_w.astype(BF16), jnp.zeros((D_MODEL,), F32), seq, n_batch)
    if debug:
        return out, (h0f, h0b, zl, dtl, xbcl)
    return out


CONV_BLOCK = 2048
CONV_PAD = 16
CONV_HALF = CONV_WIDTH // 2


def _glu_in_kernel(x_ref, shift_ref, scale_ref, nw_ref, w_ref, b_ref, u_ref):
    d = x_ref.shape[-1]
    hb = _modnorm(x_ref[...], nw_ref[...], shift_ref[...], scale_ref[...]).astype(BF16)
    for j in range(d // FFN_CHUNK):
        ca = slice(j * FFN_CHUNK, (j + 1) * FFN_CHUNK)
        cg = slice(d + j * FFN_CHUNK, d + (j + 1) * FFN_CHUNK)
        a = jnp.dot(hb, w_ref[:, ca], preferred_element_type=F32) + b_ref[:, ca]
        g = jnp.dot(hb, w_ref[:, cg], preferred_element_type=F32) + b_ref[:, cg]
        u_ref[:, ca] = (a * jax.nn.sigmoid(g)).astype(BF16)


def conv_glu_in(xs, n_tok, mods, nw, w, b, seq, n_batch):
    d = xs.shape[-1]
    tm = FFN_TM
    tpb = seq // tm
    const = lambda i: (0, 0)
    return pl.pallas_call(
        _glu_in_kernel,
        grid=(n_tok // tm,),
        in_specs=[
            pl.BlockSpec((tm, d), lambda i: (i, 0)),
            _mod_spec(3, tpb, n_batch), _mod_spec(4, tpb, n_batch),
            pl.BlockSpec((1, d), const),
            pl.BlockSpec((d, 2 * d), const, pipeline_mode=pl.Buffered(1)),
            pl.BlockSpec((1, 2 * d), const),
        ],
        out_specs=pl.BlockSpec((tm, d), lambda i: (i, 0)),
        out_shape=jax.ShapeDtypeStruct((n_tok, d), BF16),
        compiler_params=pltpu.CompilerParams(
            dimension_semantics=("arbitrary",), vmem_limit_bytes=VMEM_LIMIT_BYTES),
        name="conv_glu_in",
    )(xs, mods, mods, nw.reshape(1, d), w, b.reshape(1, 2 * d))


def _dwconv_rows(u_ref, w_ref, o_ref, pad_ref, row_len):
    n_rows = CONV_BLOCK // row_len
    pitch = row_len + CONV_PAD
    zeros = jnp.zeros((CONV_PAD, LANES), F32)
    pad_ref[0:CONV_PAD, :] = zeros
    for r in range(n_rows):
        base = CONV_PAD + r * pitch
        pad_ref[base:base + row_len, :] = u_ref[r * row_len:(r + 1) * row_len, :].astype(F32)
        pad_ref[base + row_len:base + pitch, :] = zeros
    piece = min(row_len, 64)

    def body(i, carry):
        r = i // (row_len // piece)
        p = i % (row_len // piece)
        src = CONV_PAD + r * pitch + p * piece - CONV_HALF
        acc = pad_ref[pl.ds(src, piece, stride=1), :] * w_ref[0:1, :]
        for k in range(1, CONV_WIDTH):
            acc = acc + pad_ref[pl.ds(src + k, piece, stride=1), :] * w_ref[k:k + 1, :]
        o_ref[pl.ds(pl.multiple_of(i * piece, piece), piece), :] = acc.astype(BF16)
        return carry

    lax.fori_loop(0, CONV_BLOCK // piece, body, 0)


def _dwconv_cols(u_ref, w_ref, o_ref, pad_ref):
    halo = CONV_HALF * GRID_W
    pad_ref[0:halo, :] = jnp.zeros((halo, LANES), F32)
    pad_ref[halo:halo + CONV_BLOCK, :] = u_ref[...].astype(F32)
    pad_ref[halo + CONV_BLOCK:2 * halo + CONV_BLOCK, :] = jnp.zeros((halo, LANES), F32)

    def body(r, carry):
        src = pl.multiple_of(r * GRID_W, GRID_W)
        acc = pad_ref[pl.ds(src, GRID_W), :] * w_ref[0:1, :]
        for k in range(1, CONV_WIDTH):
            acc = acc + pad_ref[pl.ds(src + k * GRID_W, GRID_W), :] * w_ref[k:k + 1, :]
        o_ref[pl.ds(src, GRID_W), :] = acc.astype(BF16)
        return carry

    lax.fori_loop(0, CONV_BLOCK // GRID_W, body, 0)


def _dwconv_kernel(u_ref, w_ref, o_ref, pad_ref, *, n_img, ctx_len):
    blk = pl.program_id(0)
    slab = pl.program_id(1)
    half_slabs = D_MODEL // LANES // 2

    @pl.when(jnp.logical_and(blk < n_img, slab < half_slabs))
    def _():
        _dwconv_rows(u_ref, w_ref, o_ref, pad_ref, GRID_W)

    @pl.when(jnp.logical_and(blk < n_img, slab >= half_slabs))
    def _():
        _dwconv_cols(u_ref, w_ref, o_ref, pad_ref)

    if ctx_len is not None:
        @pl.when(blk >= n_img)
        def _():
            _dwconv_rows(u_ref, w_ref, o_ref, pad_ref, ctx_len)


def dwconv(u, dw_w, n_img, ctx_len):
    n_tok, d = u.shape
    pad_rows = max(CONV_PAD + (CONV_BLOCK // GRID_W) * (GRID_W + CONV_PAD), CONV_BLOCK + 2 * CONV_HALF * GRID_W)
    wpad = jnp.concatenate([dw_w, jnp.zeros((1, d), F32)], axis=0)
    return pl.pallas_call(
        functools.partial(_dwconv_kernel, n_img=n_img, ctx_len=ctx_len),
        grid=(n_tok // CONV_BLOCK, d // LANES),
        in_specs=[
            pl.BlockSpec((CONV_BLOCK, LANES), lambda b, s: (b, s)),
            pl.BlockSpec((CONV_WIDTH + 1, LANES), lambda b, s: (0, s)),
        ],
        out_specs=pl.BlockSpec((CONV_BLOCK, LANES), lambda b, s: (b, s)),
        out_shape=jax.ShapeDtypeStruct((n_tok, d), BF16),
        scratch_shapes=[pltpu.VMEM((pad_rows, LANES), F32)],
        compiler_params=pltpu.CompilerParams(
            dimension_semantics=("arbitrary", "arbitrary"), vmem_limit_bytes=VMEM_LIMIT_BYTES),
        name="dwconv",
    )(u, wpad)


def _ln_proj_res_kernel(v_ref, x_ref, gate_ref, cb_ref, lw_ref, lb_ref, w_ref, b_ref, o_ref):
    v = v_ref[...].astype(F32) + cb_ref[...]
    mu = jnp.mean(v, axis=-1, keepdims=True)
    vc = v - mu
    var = jnp.mean(vc * vc, axis=-1, keepdims=True)
    a = _silu(vc * lax.rsqrt(var + EPS) * lw_ref[...] + lb_ref[...]).astype(BF16)
    y = jnp.dot(a, w_ref[...], preferred_element_type=F32) + b_ref[...]
    o_ref[...] = x_ref[...] + gate_ref[...] * y


def conv_ln_out(v, xs, mods, conv_b, ln_w, ln_b, w, b, seq, n_batch):
    n_tok, d = v.shape
    tm = FFN_TM
    tpb = seq // tm
    const = lambda i: (0, 0)
    vec = pl.BlockSpec((1, d), const)
    return pl.pallas_call(
        _ln_proj_res_kernel,
        grid=(n_tok // tm,),
        in_specs=[
            pl.BlockSpec((tm, d), lambda i: (i, 0)),
            pl.BlockSpec((tm, d), lambda i: (i, 0)),
            _mod_spec(5, tpb, n_batch),
            vec, vec, vec,
            pl.BlockSpec((d, d), const, pipeline_mode=pl.Buffered(1)),
            vec,
        ],
        out_specs=pl.BlockSpec((tm, d), lambda i: (i, 0)),
        out_shape=jax.ShapeDtypeStruct((n_tok, d), F32),
        compiler_params=pltpu.CompilerParams(
            dimension_semantics=("arbitrary",), vmem_limit_bytes=VMEM_LIMIT_BYTES),
        name="conv_ln_out",
    )(v, xs, mods, conv_b.reshape(1, d), ln_w.reshape(1, d), ln_b.reshape(1, d), w, b.reshape(1, d))


def conv_layer(xs, n_tok, n_lat, seq, clen, n_batch, mods, nw, pw1_w, pw1_b, dw_w, dw_b, ln_w, ln_b, pw2_w, pw2_b):
    assert seq == CONV_BLOCK and CONV_BLOCK % clen == 0 and (n_tok - n_lat) % CONV_BLOCK == 0
    u = conv_glu_in(xs, n_tok, mods, nw, pw1_w.astype(BF16), pw1_b, seq, n_batch)
    v = dwconv(u, dw_w, n_lat // CONV_BLOCK, clen if n_tok > n_lat else None)
    return conv_ln_out(v, xs, mods, dw_b, ln_w, ln_b, pw2_w.astype(BF16), pw2_b, seq, n_batch)


def _gelu(v):
    return 0.5 * v * (1.0 + lax.erf(v * (1.0 / math.sqrt(2.0))))


def _cmlp_kernel(x_ref, shift_ref, scale_ref, gate_ref, nw_ref, win_ref, bin_ref, lw_ref, lb_ref, spw_ref, spb_ref,
                 wout_ref, bout_ref, o_ref, u_ref, v_ref):
    x = x_ref[...]
    tm = x.shape[0]
    hb = _modnorm(x, nw_ref[...], shift_ref[...], scale_ref[...]).astype(BF16)
    for j in range(CMLP_E // FFN_CHUNK):
        cu = slice(j * FFN_CHUNK, (j + 1) * FFN_CHUNK)
        cv = slice(CMLP_E + j * FFN_CHUNK, CMLP_E + (j + 1) * FFN_CHUNK)
        u_ref[:, cu] = _gelu(jnp.dot(hb, win_ref[:, cu], preferred_element_type=F32) + bin_ref[:, cu]).astype(BF16)
        v_ref[:, cu] = _gelu(jnp.dot(hb, win_ref[:, cv], preferred_element_type=F32) + bin_ref[:, cv])
    v = v_ref[...]
    mu = jnp.mean(v, axis=-1, keepdims=True)
    vc = v - mu
    var = jnp.mean(vc * vc, axis=-1, keepdims=True)
    v_ref[...] = vc * lax.rsqrt(var + EPS) * lw_ref[...] + lb_ref[...]
    for c in range(tm // CHUNK):
        rows = slice(c * CHUNK, (c + 1) * CHUNK)
        for g in range(CMLP_GROUPS):
            cols = slice(g * CMLP_GD, (g + 1) * CMLP_GD)
            sv = jnp.dot(spw_ref[g], v_ref[rows, cols].astype(BF16), preferred_element_type=F32) + spb_ref[g]
            u_ref[rows, cols] = (u_ref[rows, cols].astype(F32) * sv).astype(BF16)
    y = jnp.dot(u_ref[...], wout_ref[...], preferred_element_type=F32) + bout_ref[...]
    o_ref[...] = x + gate_ref[...] * y


def cmlp_layer(xs, n_tok, seq, n_batch, mods, nw, in_w, in_b, ln_w, ln_b, sp_w, sp_b, out_w, out_b):
    d = xs.shape[-1]
    tm = FFN_TM
    tpb = seq // tm
    const = lambda i: (0, 0)
    const3 = lambda i: (0, 0, 0)
    spb = jnp.broadcast_to(sp_b[:, :, None], (CMLP_GROUPS, CHUNK, CMLP_GD))
    return pl.pallas_call(
        _cmlp_kernel,
        grid=(n_tok // tm,),
        in_specs=[
            pl.BlockSpec((tm, d), lambda i: (i, 0)),
            _mod_spec(3, tpb, n_batch), _mod_spec(4, tpb, n_batch), _mod_spec(5, tpb, n_batch),
            pl.BlockSpec((1, d), const),
            pl.BlockSpec((d, 2 * CMLP_E), const, pipeline_mode=pl.Buffered(1)),
            pl.BlockSpec((1, 2 * CMLP_E), const),
            pl.BlockSpec((1, CMLP_E), const), pl.BlockSpec((1, CMLP_E), const),
            pl.BlockSpec((CMLP_GROUPS, CHUNK, CHUNK), const3),
            pl.BlockSpec((CMLP_GROUPS, CHUNK, CMLP_GD), const3, pipeline_mode=pl.Buffered(1)),
            pl.BlockSpec((CMLP_E, d), const, pipeline_mode=pl.Buffered(1)),
            pl.BlockSpec((1, d), const),
        ],
        out_specs=pl.BlockSpec((tm, d), lambda i: (i, 0)),
        out_shape=jax.ShapeDtypeStruct((n_tok, d), F32),
        scratch_shapes=[pltpu.VMEM((tm, CMLP_E), BF16), pltpu.VMEM((tm, CMLP_E), F32)],
        compiler_params=pltpu.CompilerParams(
            dimension_semantics=("arbitrary",), vmem_limit_bytes=VMEM_LIMIT_BYTES),
        name="cmlp_layer",
    )(xs, mods, mods, mods, nw.reshape(1, d), in_w.astype(BF16), in_b.reshape(1, 2 * CMLP_E),
      ln_w.reshape(1, CMLP_E), ln_b.reshape(1, CMLP_E), sp_w.astype(BF16), spb,
      out_w.astype(BF16), out_b.reshape(1, d))


def kernel(x, c, ctx, c_ctx, ada_w, ada_b, norm_w, ffn_w_in, ffn_w_out, conv_pw1_w, conv_pw1_b, conv_dw_w, conv_dw_b, conv_ln_w, conv_ln_b, conv_pw2_w, conv_pw2_b, ssd_in_w, ssd_conv_w, ssd_conv_b, ssd_dt_bias, ssd_a_log, ssd_d, ssd_norm_w, ssd_out_w, cmlp_in_w, cmlp_in_b, cmlp_ln_w, cmlp_ln_b, cmlp_sp_w, cmlp_sp_b, cmlp_out_w, cmlp_out_b, final_norm_w):
    bsz, seq, d = x.shape
    clen = ctx.shape[1]
    n_lat = bsz * seq
    n_all = n_lat + bsz * clen
    last_ssd = max((i for i in range(DEPTH) if i % N_MIXERS == 1), default=-1)

    cvec = jnp.concatenate([c, c_ctx[None, :], jnp.zeros((MOD_ROWS - bsz - 1, d), F32)], axis=0)
    mods_all = ada_modulation(cvec, ada_w, ada_b)
    w_in_b = ffn_w_in.astype(BF16)
    w_out_b = ffn_w_out.astype(BF16)

    xs = jnp.concatenate([x.reshape(n_lat, d), ctx.reshape(bsz * clen, d)], axis=0)
    for i in range(DEPTH):
        kind, j = i % N_MIXERS, i // N_MIXERS
        ctx_live = i <= last_ssd
        ctx_carry = i < last_ssd
        n_mix = n_all if ctx_carry else n_lat
        mods = mods_all[i].reshape(MOD_ROWS * N_MODS, 1, d)

        xs = ffn_block(xs, n_all if ctx_live else n_lat, mods, 0, norm_w[i, 0], w_in_b[i, 0], w_out_b[i, 0],
                       seq, bsz)
        if kind == 0:
            xs = conv_layer(xs, n_mix, n_lat, seq, clen, bsz, mods, norm_w[i, 1], conv_pw1_w[j], conv_pw1_b[j],
                            conv_dw_w[j], conv_dw_b[j], conv_ln_w[j], conv_ln_b[j], conv_pw2_w[j], conv_pw2_b[j])
        elif kind == 1:
            assert ctx_live and not ctx_carry, "context is only carried up to the last SSD layer"
            xs = ssd_layer(xs, n_lat, n_all - n_lat, seq, clen, bsz, mods, norm_w[i, 1], ssd_in_w[j], ssd_conv_w[j],
                           ssd_conv_b[j], ssd_dt_bias[j], ssd_a_log[j], ssd_d[j], ssd_norm_w[j], ssd_out_w[j])
        else:
            xs = cmlp_layer(xs, n_mix, seq, bsz, mods, norm_w[i, 1], cmlp_in_w[j], cmlp_in_b[j], cmlp_ln_w[j],
                            cmlp_ln_b[j], cmlp_sp_w[j], cmlp_sp_b[j], cmlp_out_w[j], cmlp_out_b[j])
        xs = ffn_block(xs, n_mix, mods, 6, norm_w[i, 2], w_in_b[i, 1], w_out_b[i, 1], seq, bsz,
                       final_w=final_norm_w if i == DEPTH - 1 else None)
    return xs.reshape(bsz, seq, d)
```

```python
import functools
import math

import jax
import jax.numpy as jnp
from jax import lax
from jax.experimental import pallas as pl
from jax.experimental.pallas import tpu as pltpu

D_MODEL = 1024
DEPTH = 4
GRID_W = 64
N_MIXERS = 3
N_MODS = 9
EPS = 1e-6
D_FF = 2816
CONV_WIDTH = 31
D_INNER = 2 * D_MODEL
SSD_HEADDIM = 64
SSD_HEADS = D_INNER // SSD_HEADDIM
SSD_GROUPS = 8
SSD_HPG = SSD_HEADS // SSD_GROUPS
D_STATE = 128
SSD_CONV = 7
SSD_CHUNK = 128
SSD_GN = SSD_GROUPS * D_STATE
SSD_CONV_DIM = D_INNER + 2 * SSD_GN
CHUNK = 128
CMLP_E = 2 * D_MODEL
CMLP_GROUPS = 8
CMLP_GD = CMLP_E // CMLP_GROUPS

SUBLANES = 8
LANES = 128
VMEM_LIMIT_BYTES = 56 * 1024 * 1024

MOD_ROWS = 24
FFN_TM = 1024
FFN_TM_FUSED = 512
TOKEN_TM = 512
FFN_CHUNK = 256
BF16 = jnp.bfloat16
F32 = jnp.float32


def _silu(v):
    return v * jax.nn.sigmoid(v)


def _ada_kernel(c_ref, w_ref, b_ref, o_ref):
    s = _silu(c_ref[...]).astype(BF16)
    o_ref[...] = jnp.dot(s, w_ref[...].astype(BF16), preferred_element_type=F32) + b_ref[...]


def ada_modulation(cvec, ada_w, ada_b):
    depth, d, n = ada_w.shape
    tn = d
    return pl.pallas_call(
        _ada_kernel,
        grid=(depth, n // tn),
        in_specs=[
            pl.BlockSpec((MOD_ROWS, d), lambda l, j: (0, 0)),
            pl.BlockSpec((None, d, tn), lambda l, j: (l, 0, j)),
            pl.BlockSpec((None, 1, tn), lambda l, j: (l, 0, j)),
        ],
        out_specs=pl.BlockSpec((None, MOD_ROWS, tn), lambda l, j: (l, 0, j)),
        out_shape=jax.ShapeDtypeStruct((depth, MOD_ROWS, n), F32),
        compiler_params=pltpu.CompilerParams(
            dimension_semantics=("arbitrary", "arbitrary"), vmem_limit_bytes=VMEM_LIMIT_BYTES),
        name="ada_modulation",
    )(cvec, ada_w, ada_b.reshape(depth, 1, n))


def _modnorm(x, nw, shift, scale):
    xn = x * lax.rsqrt(jnp.mean(x * x, axis=-1, keepdims=True) + EPS) * nw
    return xn * (1.0 + scale) + shift


def _ffn_kernel(*refs, names, n_x_tiles):
    r = dict(zip(names, refs))
    x = r["x"][...]
    if "x2" in r:
        x = jnp.where(pl.program_id(0) < n_x_tiles, x, r["x2"][...])
    if "mix_a" in r:
        a = r["mix_a"][...]
        if "mix_lw" in r:
            v = a.astype(F32) + r["mix_cb"][...]
            vc = v - jnp.mean(v, axis=-1, keepdims=True)
            var = jnp.mean(vc * vc, axis=-1, keepdims=True)
            a = _silu(vc * lax.rsqrt(var + EPS) * r["mix_lw"][...] + r["mix_lb"][...]).astype(BF16)
        y0 = jnp.dot(a, r["mix_w"][...], preferred_element_type=F32) + r["mix_b"][...]
        x = x + r["mix_gate"][...] * y0
    hb = _modnorm(x, r["nw"][...], r["shift"][...], r["scale"][...]).astype(BF16)
    win_ref, wout_ref, a_ref = r["win"], r["wout"], r["a_scr"]
    for j in range(D_FF // FFN_CHUNK):
        lo = j * FFN_CHUNK
        g = jnp.dot(hb, win_ref[:, lo:lo + FFN_CHUNK], preferred_element_type=F32)
        u = jnp.dot(hb, win_ref[:, D_FF + lo:D_FF + lo + FFN_CHUNK], preferred_element_type=F32)
        a_ref[:, lo:lo + FFN_CHUNK] = (_silu(g) * u).astype(BF16)
    y = jnp.dot(a_ref[...], wout_ref[...], preferred_element_type=F32)
    out = x + (0.5 * r["gate"][...]) * y
    if "glu_w" in r:
        d = out.shape[-1]
        h2 = _modnorm(out, r["glu_nw"][...], r["glu_shift"][...], r["glu_scale"][...]).astype(BF16)
        for j in range(d // FFN_CHUNK):
            ca = slice(j * FFN_CHUNK, (j + 1) * FFN_CHUNK)
            cg = slice(d + j * FFN_CHUNK, d + (j + 1) * FFN_CHUNK)
            ga = jnp.dot(h2, r["glu_w"][:, ca], preferred_element_type=F32) + r["glu_b"][:, ca]
            gg = jnp.dot(h2, r["glu_w"][:, cg], preferred_element_type=F32) + r["glu_b"][:, cg]
            r["u"][:, ca] = (ga * jax.nn.sigmoid(gg)).astype(BF16)
    if "fw" in r:
        out = out * lax.rsqrt(jnp.mean(out * out, axis=-1, keepdims=True) + EPS) * r["fw"][...]
    r["o"][...] = out


def _mod_spec(m, tiles_per_batch, n_batch):
    return pl.BlockSpec(
        (None, 1, D_MODEL),
        lambda i: (jnp.minimum(i // tiles_per_batch, n_batch) * N_MODS + m, 0, 0))


def ffn_block(xs, n_tokens, mods, m0, nw, w_in, w_out, seq, n_batch, xs2=None, mix=None, glu=None, final_w=None):
    d = xs.shape[-1]
    tm = FFN_TM if (mix is None and glu is None) else FFN_TM_FUSED
    tpb = seq // tm
    n1 = xs.shape[0] // tm
    assert seq % tm == 0 and n_tokens % tm == 0 and xs.shape[0] % tm == 0, "token counts must be whole tiles"
    const = lambda i: (0, 0)
    vec = lambda n: pl.BlockSpec((1, n), const)
    tile = lambda n: pl.BlockSpec((tm, n), lambda i: (i, 0))
    resident = lambda shape: pl.BlockSpec(shape, const, pipeline_mode=pl.Buffered(1))
    mod = lambda m: _mod_spec(m, tpb, n_batch)
    ins = []
    if xs2 is None:
        ins.append(("x", tile(d), xs))
    else:
        ins.append(("x", pl.BlockSpec((tm, d), lambda i: (jnp.minimum(i, n1 - 1), 0)), xs))
        ins.append(("x2", pl.BlockSpec((tm, d), lambda i: (jnp.maximum(i - n1, 0), 0)), xs2))
    if mix is not None:
        a, w, b = mix[:3]
        ins += [("mix_a", tile(a.shape[1]), a), ("mix_gate", mod(5), mods),
                ("mix_w", resident(w.shape), w), ("mix_b", vec(d), b.reshape(1, d))]
        if len(mix) > 3:
            ins += [(k, vec(a.shape[1]), t.reshape(1, -1)) for k, t in zip(("mix_cb", "mix_lw", "mix_lb"), mix[3:])]
    ins += [("shift", mod(m0), mods), ("scale", mod(m0 + 1), mods), ("gate", mod(m0 + 2), mods),
            ("nw", vec(d), nw.reshape(1, d)),
            ("win", resident((d, 2 * D_FF)), w_in), ("wout", resident((D_FF, d)), w_out)]
    outs = [("o", tile(d), jax.ShapeDtypeStruct((n_tokens, d), F32))]
    if glu is not None:
        gnw, gw, gb = glu
        ins += [("glu_shift", mod(3), mods), ("glu_scale", mod(4), mods), ("glu_nw", vec(d), gnw.reshape(1, d)),
                ("glu_w", resident(gw.shape), gw), ("glu_b", vec(gw.shape[1]), gb.reshape(1, -1))]
        outs.append(("u", tile(d), jax.ShapeDtypeStruct((n_tokens, d), BF16)))
    if final_w is not None:
        ins.append(("fw", vec(d), final_w.reshape(1, d)))
    names = tuple(n for n, _, _ in ins) + tuple(n for n, _, _ in outs) + ("a_scr",)
    res = pl.pallas_call(
        functools.partial(_ffn_kernel, names=names, n_x_tiles=n1),
        grid=(n_tokens // tm,),
        in_specs=[s for _, s, _ in ins],
        out_specs=[s for _, s, _ in outs],
        out_shape=[s for _, _, s in outs],
        scratch_shapes=[pltpu.VMEM((tm, D_FF), BF16)],
        compiler_params=pltpu.CompilerParams(
            dimension_semantics=("arbitrary",), vmem_limit_bytes=VMEM_LIMIT_BYTES),
        name="ffn_block",
    )(*[t for _, _, t in ins])
    return res[0] if glu is None else tuple(res)


SSD_HALO = SUBLANES
SSD_NEG = -1e30
SSD_XBC_CHUNK = 512
SSD_STEP_CHUNKS = 2
LOG2E = 1.4426950408889634
HP = D_INNER
GHP = D_INNER // SSD_GROUPS


def _softplus(v):
    return jnp.maximum(v, 0.0) + jnp.log1p(jnp.exp(-jnp.abs(v)))


def _ssd_proj_kernel(xp_ref, x_ref, xn_ref, shift_ref, scale_ref, nw_ref, wz_ref, wdt_ref, wxbc_ref,
                     cw_ref, cb_ref, dtb_ref, z_ref, dt_ref, xbc_ref, ext_ref, *, tiles_per_seq, tm):
    pos = pl.program_id(0) % tiles_per_seq
    nw, shift, scale = nw_ref[...], shift_ref[...], scale_ref[...]
    h = _modnorm(x_ref[...], nw, shift, scale)
    hp = _modnorm(xp_ref[...], nw, shift, scale) * (pos > 0).astype(F32)
    hn = _modnorm(xn_ref[...], nw, shift, scale) * (pos < tiles_per_seq - 1).astype(F32)
    hb = h.astype(BF16)
    hext = jnp.concatenate([hp, h, hn], axis=0).astype(BF16)

    for j in range(D_INNER // SSD_XBC_CHUNK):
        cols = slice(j * SSD_XBC_CHUNK, (j + 1) * SSD_XBC_CHUNK)
        z_ref[:, cols] = jnp.dot(hb, wz_ref[:, cols], preferred_element_type=F32).astype(BF16)
    dt_ref[...] = _softplus(jnp.dot(hb, wdt_ref[...], preferred_element_type=F32) + dtb_ref[...])

    off = SSD_HALO - SSD_CONV // 2
    slabs_per_chunk = SSD_XBC_CHUNK // LANES
    for j in range(SSD_CONV_DIM // SSD_XBC_CHUNK):
        cols = slice(j * SSD_XBC_CHUNK, (j + 1) * SSD_XBC_CHUNK)
        e = jnp.dot(hext, wxbc_ref[:, cols], preferred_element_type=F32)
        for s in range(slabs_per_chunk):
            sl = j * slabs_per_chunk + s
            lc = slice(sl * LANES, (sl + 1) * LANES)
            ext_ref[sl] = e[:, s * LANES:(s + 1) * LANES]
            acc = ext_ref[sl, pl.ds(off, tm, stride=1), :] * cw_ref[0:1, lc] + cb_ref[:, lc]
            for k in range(1, SSD_CONV):
                acc = acc + ext_ref[sl, pl.ds(off + k, tm, stride=1), :] * cw_ref[k:k + 1, lc]
            xbc_ref[:, lc] = _silu(acc).astype(BF16)


def ssd_proj(xs, tok0, n_tok, seq, tm, mods, mod_row, nw, w_z, w_dt, w_xbc, conv_w, conv_b, dt_bias128):
    d = xs.shape[-1]
    tps = seq // tm
    t0 = tok0 // tm
    hb = tm // SSD_HALO
    n_hblk = xs.shape[0] // SSD_HALO
    const = lambda i: (0, 0)
    mspec = lambda m: pl.BlockSpec((None, 1, d), lambda i: (mod_row(i) * N_MODS + m, 0, 0))
    return pl.pallas_call(
        functools.partial(_ssd_proj_kernel, tiles_per_seq=tps, tm=tm),
        grid=(n_tok // tm,),
        in_specs=[
            pl.BlockSpec((SSD_HALO, d), lambda i: (jnp.maximum((i + t0) * hb - 1, 0), 0)),
            pl.BlockSpec((tm, d), lambda i: (i + t0, 0)),
            pl.BlockSpec((SSD_HALO, d), lambda i: (jnp.minimum((i + t0 + 1) * hb, n_hblk - 1), 0)),
            mspec(3), mspec(4),
            pl.BlockSpec((1, d), const),
            pl.BlockSpec((d, D_INNER), const, pipeline_mode=pl.Buffered(1)),
            pl.BlockSpec((d, 128), const, pipeline_mode=pl.Buffered(1)),
            pl.BlockSpec((d, SSD_CONV_DIM), const, pipeline_mode=pl.Buffered(1)),
            pl.BlockSpec((SSD_CONV + 1, SSD_CONV_DIM), const),
            pl.BlockSpec((1, SSD_CONV_DIM), const),
            pl.BlockSpec((1, 128), const),
        ],
        out_specs=[
            pl.BlockSpec((tm, D_INNER), lambda i: (i, 0)),
            pl.BlockSpec((tm, 128), lambda i: (i, 0)),
            pl.BlockSpec((tm, SSD_CONV_DIM), lambda i: (i, 0)),
        ],
        out_shape=[
            jax.ShapeDtypeStruct((n_tok, D_INNER), BF16),
            jax.ShapeDtypeStruct((n_tok, 128), F32),
            jax.ShapeDtypeStruct((n_tok, SSD_CONV_DIM), BF16),
        ],
        scratch_shapes=[pltpu.VMEM((SSD_CONV_DIM // LANES, tm + 2 * SSD_HALO, LANES), F32)],
        compiler_params=pltpu.CompilerParams(
            dimension_semantics=("arbitrary",), vmem_limit_bytes=VMEM_LIMIT_BYTES),
        name="ssd_proj",
    )(xs, xs, xs, mods, mods, nw.reshape(1, d), w_z, w_dt, w_xbc,
      jnp.concatenate([conv_w, jnp.zeros((1, SSD_CONV_DIM), F32)], axis=0), conv_b.reshape(1, SSD_CONV_DIM),
      dt_bias128)


def _split3(v):
    hi = v.astype(BF16)
    r = v - hi.astype(F32)
    mid = r.astype(BF16)
    lo = (r - mid.astype(F32)).astype(BF16)
    return hi, mid, lo


def _sum3(p, n, axis):
    if axis == 1:
        return p[:, :n] + p[:, n:2 * n] + p[:, 2 * n:]
    return p[:n] + p[n:2 * n] + p[2 * n:]


def _chunk_decays(dt128, alog128, with_rows):
    q = dt128.shape[0]
    ri = lax.broadcasted_iota(jnp.int32, (q, q), 0)
    ci = lax.broadcasted_iota(jnp.int32, (q, q), 1)
    ltri = (ci <= ri).astype(BF16)
    utri = (ci >= ri).astype(BF16)
    lane = lax.broadcasted_iota(jnp.int32, (1, LANES), 1)
    lane_fwd = (lane % (2 * SSD_HEADS)) < SSD_HEADS
    v = dt128 * (-jnp.exp(alog128) * LOG2E)
    v3 = jnp.concatenate(_split3(v), axis=1)
    csl = _sum3(jnp.dot(ltri, v3, preferred_element_type=F32), LANES, 1)
    csu = _sum3(jnp.dot(utri, v3, preferred_element_type=F32), LANES, 1)
    cs = jnp.where(lane_fwd, csl, csu)
    edge = jnp.where(lane_fwd, csl[q - 1:q, :], csu[0:1, :])
    qm = jnp.where(lane < 2 * SSD_HEADS, jnp.exp2(cs), jnp.exp2(edge - cs) * dt128)
    if not with_rows:
        return cs, qm, None, None
    nh = 2 * SSD_HEADS
    vt3 = jnp.concatenate(_split3(v.T[:nh]), axis=0)
    p = _sum3(jnp.dot(vt3, jnp.concatenate([utri, ltri], axis=1), preferred_element_type=F32), nh, 0)
    row = lax.broadcasted_iota(jnp.int32, (nh, 1), 0)
    cst = jnp.where(row < SSD_HEADS, p[:, :q], p[:, q:])
    return cs, qm, cst, dt128.T[:nh]


def _split2(qm):
    hi = qm.astype(BF16)
    lo = (qm - hi.astype(F32)).astype(BF16)
    return jnp.concatenate([hi, lo], axis=1)


def _expand_heads(qq, e_mat):
    return jnp.dot(qq, e_mat, preferred_element_type=F32)


def _group_state(bg, xg, wg):
    return lax.dot_general(bg, xg * wg.astype(BF16), (((0,), (0,)), ((), ())), preferred_element_type=F32)


def _ssd_states_kernel(xf_ref, bf_ref, dtf_ref, xb_ref, bb_ref, dtb_ref, alog_ref, e_ref, hf_ref, hb_ref):
    s = pl.program_id(1)

    @pl.when(s == 0)
    def _():
        hf_ref[...] = jnp.zeros_like(hf_ref)
        hb_ref[...] = jnp.zeros_like(hb_ref)

    for x_ref, b_ref, dt_ref, st_ref, k_w, k_dec, edge_row in (
            (xf_ref, bf_ref, dtf_ref, hf_ref, 2, 0, SSD_CHUNK - 1), (xb_ref, bb_ref, dtb_ref, hb_ref, 3, 1, 0)):
        _, qm, _, _ = _chunk_decays(dt_ref[...], alog_ref[...], False)
        qq = _split2(qm)
        w = _expand_heads(qq, e_ref[k_w])
        dec = _expand_heads(qq, e_ref[k_dec])[edge_row:edge_row + 1, :]
        x = x_ref[...]
        bm = b_ref[...]
        for g in range(SSD_GROUPS):
            gl = slice(g * GHP, (g + 1) * GHP)
            st = _group_state(bm[:, g * D_STATE:(g + 1) * D_STATE], x[:, gl], w[:, gl])
            st_ref[g] = st_ref[g] * dec[:, gl] + st


def ssd_context_states(xbc, dt, alog128, e_all, n_batch, seq):
    nc = seq // SSD_CHUNK
    q = SSD_CHUNK
    st_shape = jax.ShapeDtypeStruct((n_batch, SSD_GROUPS, D_STATE, GHP), F32)
    st_spec = pl.BlockSpec((None, SSD_GROUPS, D_STATE, GHP), lambda b, s: (b, 0, 0, 0))
    fwd = lambda b, s: b * nc + s
    bwd = lambda b, s: b * nc + nc - 1 - s
    return pl.pallas_call(
        _ssd_states_kernel,
        grid=(n_batch, nc),
        in_specs=[
            pl.BlockSpec((q, HP), lambda b, s: (fwd(b, s), 0)),
            pl.BlockSpec((q, SSD_GN), lambda b, s: (fwd(b, s), HP // SSD_GN)),
            pl.BlockSpec((q, 128), lambda b, s: (fwd(b, s), 0)),
            pl.BlockSpec((q, HP), lambda b, s: (bwd(b, s), 0)),
            pl.BlockSpec((q, SSD_GN), lambda b, s: (bwd(b, s), HP // SSD_GN)),
            pl.BlockSpec((q, 128), lambda b, s: (bwd(b, s), 0)),
            pl.BlockSpec((1, 128), lambda b, s: (0, 0)),
            pl.BlockSpec((4, 256, HP), lambda b, s: (0, 0, 0), pipeline_mode=pl.Buffered(1)),
        ],
        out_specs=[st_spec, st_spec],
        out_shape=[st_shape, st_shape],
        compiler_params=pltpu.CompilerParams(
            dimension_semantics=("arbitrary", "arbitrary"), vmem_limit_bytes=VMEM_LIMIT_BYTES),
        name="ssd_context_states",
    )(xbc, xbc, dt, xbc, xbc, dt, alog128, e_all)


def _ssd_bwd_chunk(x, bm, dt128, alog128, e_ref, hb_ref, hbs_ref, cc):
    _, qm, _, _ = _chunk_decays(dt128, alog128, False)
    qq = _split2(qm)
    w = _expand_heads(qq, e_ref[3])
    dec = _expand_heads(qq[0:2 * SUBLANES], e_ref[1])[0:1, :]
    for g in range(SSD_GROUPS):
        gl = slice(g * GHP, (g + 1) * GHP)
        hbs_ref[cc, g] = hb_ref[g].astype(BF16)
        st = _group_state(bm[:, g * D_STATE:(g + 1) * D_STATE], x[:, gl], w[:, gl])
        hb_ref[g] = hb_ref[g] * dec[:, gl] + st


def _ssd_fwd_chunk(x, bm, cm, z, dt128, alog128, dskip, normw, e_ref, hf_ref, hbs_ref, cc, o_ref, rows):
    q = SSD_CHUNK
    cs, qm, cst, dtt = _chunk_decays(dt128, alog128, True)
    qq = _split2(qm)
    ef = _expand_heads(qq, e_ref[0])
    eb = _expand_heads(qq, e_ref[1])
    wf = _expand_heads(qq, e_ref[2])
    dec = ef[q - 1:q, :]
    ri = lax.broadcasted_iota(jnp.int32, (q, q), 0)
    ci = lax.broadcasted_iota(jnp.int32, (q, q), 1)
    tril = ci <= ri
    triu = ci >= ri
    glane = lax.broadcasted_iota(jnp.int32, (1, GHP), 1) // SSD_HEADDIM
    for g in range(SSD_GROUPS):
        gl = slice(g * GHP, (g + 1) * GHP)
        bg = bm[:, g * D_STATE:(g + 1) * D_STATE]
        cg = cm[:, g * D_STATE:(g + 1) * D_STATE]
        xg = x[:, gl]
        cb = lax.dot_general(cg, bg, (((1,), (1,)), ((), ())), preferred_element_type=F32)
        ms, xs = [], []
        for r in range(SSD_HPG):
            hd = g * SSD_HPG + r
            hb_ = SSD_HEADS + hd
            lf = jnp.exp2(jnp.where(tril, cs[:, hd:hd + 1] - cst[hd:hd + 1, :], SSD_NEG))
            lb = jnp.exp2(jnp.where(triu, cs[:, hb_:hb_ + 1] - cst[hb_:hb_ + 1, :], SSD_NEG))
            ms.append((cb * (lf * dtt[hd:hd + 1, :] + lb * dtt[hb_:hb_ + 1, :])).astype(BF16))
            xs.append(jnp.where(glane == r, xg, jnp.zeros_like(xg)))
        y = jnp.dot(jnp.concatenate(ms, axis=1), jnp.concatenate(xs, axis=0), preferred_element_type=F32)
        y = y + jnp.dot(cg, hf_ref[g].astype(BF16), preferred_element_type=F32) * ef[:, gl]
        y = y + jnp.dot(cg, hbs_ref[cc, g], preferred_element_type=F32) * eb[:, gl]
        y = y + dskip[:, gl] * xg.astype(F32)
        yz = y * _silu(z[:, gl].astype(F32))
        yn = yz * lax.rsqrt(jnp.mean(yz * yz, axis=-1, keepdims=True) + EPS) * normw[:, gl]
        o_ref[rows, gl] = yn.astype(BF16)
        hf_ref[g] = hf_ref[g] * dec[:, gl] + _group_state(bg, xg, wf[:, gl])


def _ssd_scan_kernel(x_ref, b_ref, c_ref, z_ref, dt_ref, alog_ref, dskip_ref, normw_ref, e_ref, h0f_ref, h0b_ref,
                     o_ref, hf_ref, hb_ref, hbs_ref, *, n_steps):
    s = pl.program_id(1)
    q = SSD_CHUNK

    @pl.when(s == 0)
    def _():
        hf_ref[...] = h0f_ref[...]
        hb_ref[...] = h0b_ref[...]

    @pl.when(s < n_steps)
    def _():
        blk = n_steps - 1 - s
        for j in reversed(range(SSD_STEP_CHUNKS)):
            rows = slice(j * q, (j + 1) * q)
            _ssd_bwd_chunk(x_ref[rows, :], b_ref[rows, :], dt_ref[rows, :], alog_ref[...], e_ref, hb_ref, hbs_ref,
                           blk * SSD_STEP_CHUNKS + j)

    @pl.when(s >= n_steps)
    def _():
        blk = s - n_steps
        for j in range(SSD_STEP_CHUNKS):
            rows = slice(j * q, (j + 1) * q)
            _ssd_fwd_chunk(x_ref[rows, :], b_ref[rows, :], c_ref[rows, :], z_ref[rows, :], dt_ref[rows, :],
                           alog_ref[...], dskip_ref[...], normw_ref[...], e_ref, hf_ref, hbs_ref,
                           blk * SSD_STEP_CHUNKS + j, o_ref, rows)


def ssd_scan(xbc, z, dt, alog128, dskip_row, norm_w, e_all, h0f, h0b, n_batch, seq):
    nc = seq // SSD_CHUNK
    q = SSD_CHUNK * SSD_STEP_CHUNKS
    ns = seq // q
    chunk = lambda b, s: b * ns + jnp.where(s < ns, ns - 1 - s, s - ns)
    chunk_b = lambda b, s: b * ns + jnp.maximum(s - ns, 0)
    st_spec = pl.BlockSpec((None, SSD_GROUPS, D_STATE, GHP), lambda b, s: (b, 0, 0, 0))
    row = lambda n: pl.BlockSpec((1, n), lambda b, s: (0, 0))
    return pl.pallas_call(
        functools.partial(_ssd_scan_kernel, n_steps=ns),
        grid=(n_batch, 2 * ns),
        in_specs=[
            pl.BlockSpec((q, HP), lambda b, s: (chunk(b, s), 0)),
            pl.BlockSpec((q, SSD_GN), lambda b, s: (chunk(b, s), HP // SSD_GN)),
            pl.BlockSpec((q, SSD_GN), lambda b, s: (chunk_b(b, s), HP // SSD_GN + 1)),
            pl.BlockSpec((q, HP), lambda b, s: (chunk_b(b, s), 0)),
            pl.BlockSpec((q, 128), lambda b, s: (chunk(b, s), 0)),
            row(128), row(HP), row(HP),
            pl.BlockSpec((4, 256, HP), lambda b, s: (0, 0, 0), pipeline_mode=pl.Buffered(1)),
            st_spec, st_spec,
        ],
        out_specs=pl.BlockSpec((q, HP), lambda b, s: (chunk_b(b, s), 0)),
        out_shape=jax.ShapeDtypeStruct((n_batch * seq, HP), BF16),
        scratch_shapes=[
            pltpu.VMEM((SSD_GROUPS, D_STATE, GHP), F32),
            pltpu.VMEM((SSD_GROUPS, D_STATE, GHP), F32),
            pltpu.VMEM((nc, SSD_GROUPS, D_STATE, GHP), BF16),
        ],
        compiler_params=pltpu.CompilerParams(
            dimension_semantics=("arbitrary", "arbitrary"), vmem_limit_bytes=VMEM_LIMIT_BYTES),
        name="ssd_scan",
    )(xbc, xbc, xbc, z, dt, alog128, dskip_row, norm_w.reshape(1, HP), e_all, h0f, h0b)


def _ssd_expand_matrices():
    hd = jnp.arange(HP) // SSD_HEADDIM
    rows = jnp.arange(256)
    mats = []
    for k in range(4):
        sel = ((rows % 128) // SSD_HEADS == k)
        mats.append(((rows % SSD_HEADS)[:, None] == hd[None, :]) & sel[:, None])
    return jnp.stack(mats).astype(BF16)


def ssd_mixer(xs, n_lat, n_ctx, seq, clen, n_batch, mods, nw, in_w, conv_w, conv_b, dt_bias, a_log, d_skip, norm_w):
    w_z = in_w[:, :D_INNER].astype(BF16)
    w_dt2 = in_w[:, D_INNER:D_INNER + 2 * SSD_HEADS]
    w_dt = jnp.concatenate([w_dt2, w_dt2], axis=1).astype(BF16)
    w_xbc = in_w[:, D_INNER + 2 * SSD_HEADS:].astype(BF16)
    both = lambda t: jnp.concatenate([t.reshape(1, 2 * SSD_HEADS)] * 2, axis=1)
    dt_bias128, alog128 = both(dt_bias), both(a_log)
    dskip_row = jnp.repeat(d_skip, SSD_HEADDIM).reshape(1, HP)
    e_all = _ssd_expand_matrices()
    proj_args = (mods, nw, w_z, w_dt, w_xbc, conv_w, conv_b, dt_bias128)

    ctm = min(clen, TOKEN_TM)
    _, dtc, xbcc = ssd_proj(xs, n_lat, n_ctx, clen, ctm, proj_args[0], lambda i: n_batch, *proj_args[1:])
    h0f, h0b = ssd_context_states(xbcc, dtc, alog128, e_all, n_batch, clen)
    tpb = seq // TOKEN_TM
    zl, dtl, xbcl = ssd_proj(xs, 0, n_lat, seq, TOKEN_TM, proj_args[0], lambda i: i // tpb, *proj_args[1:])
    return ssd_scan(xbcl, zl, dtl, alog128, dskip_row, norm_w, e_all, h0f, h0b, n_batch, seq)


CONV_BLOCK = 2048
CONV_PAD = 16
CONV_HALF = CONV_WIDTH // 2


def _dwconv_rows(u_ref, w_ref, o_ref, pad_ref, row_len):
    n_rows = CONV_BLOCK // row_len
    pitch = row_len + CONV_PAD
    zeros = jnp.zeros((CONV_PAD, LANES), F32)
    pad_ref[0:CONV_PAD, :] = zeros
    for r in range(n_rows):
        base = CONV_PAD + r * pitch
        pad_ref[base:base + row_len, :] = u_ref[r * row_len:(r + 1) * row_len, :].astype(F32)
        pad_ref[base + row_len:base + pitch, :] = zeros
    piece = min(row_len, 64)

    def body(i, carry):
        r = i // (row_len // piece)
        p = i % (row_len // piece)
        src = CONV_PAD + r * pitch + p * piece - CONV_HALF
        acc = pad_ref[pl.ds(src, piece, stride=1), :] * w_ref[0:1, :]
        for k in range(1, CONV_WIDTH):
            acc = acc + pad_ref[pl.ds(src + k, piece, stride=1), :] * w_ref[k:k + 1, :]
        o_ref[pl.ds(pl.multiple_of(i * piece, piece), piece), :] = acc.astype(BF16)
        return carry

    lax.fori_loop(0, CONV_BLOCK // piece, body, 0)


def _dwconv_cols(u_ref, w_ref, o_ref, pad_ref):
    halo = CONV_HALF * GRID_W
    pad_ref[0:halo, :] = jnp.zeros((halo, LANES), F32)
    pad_ref[halo:halo + CONV_BLOCK, :] = u_ref[...].astype(F32)
    pad_ref[halo + CONV_BLOCK:2 * halo + CONV_BLOCK, :] = jnp.zeros((halo, LANES), F32)

    def body(r, carry):
        src = pl.multiple_of(r * GRID_W, GRID_W)
        acc = pad_ref[pl.ds(src, GRID_W), :] * w_ref[0:1, :]
        for k in range(1, CONV_WIDTH):
            acc = acc + pad_ref[pl.ds(src + k * GRID_W, GRID_W), :] * w_ref[k:k + 1, :]
        o_ref[pl.ds(src, GRID_W), :] = acc.astype(BF16)
        return carry

    lax.fori_loop(0, CONV_BLOCK // GRID_W, body, 0)


def _dwconv_kernel(u_ref, w_ref, o_ref, pad_ref, *, n_img, ctx_len):
    blk = pl.program_id(0)
    slab = pl.program_id(1)
    half_slabs = D_MODEL // LANES // 2

    @pl.when(jnp.logical_and(blk < n_img, slab < half_slabs))
    def _():
        _dwconv_rows(u_ref, w_ref, o_ref, pad_ref, GRID_W)

    @pl.when(jnp.logical_and(blk < n_img, slab >= half_slabs))
    def _():
        _dwconv_cols(u_ref, w_ref, o_ref, pad_ref)

    if ctx_len is not None:
        @pl.when(blk >= n_img)
        def _():
            _dwconv_rows(u_ref, w_ref, o_ref, pad_ref, ctx_len)


def dwconv(u, dw_w, n_img, ctx_len):
    n_tok, d = u.shape
    pad_rows = max(CONV_PAD + (CONV_BLOCK // GRID_W) * (GRID_W + CONV_PAD), CONV_BLOCK + 2 * CONV_HALF * GRID_W)
    wpad = jnp.concatenate([dw_w, jnp.zeros((1, d), F32)], axis=0)
    return pl.pallas_call(
        functools.partial(_dwconv_kernel, n_img=n_img, ctx_len=ctx_len),
        grid=(n_tok // CONV_BLOCK, d // LANES),
        in_specs=[
            pl.BlockSpec((CONV_BLOCK, LANES), lambda b, s: (b, s)),
            pl.BlockSpec((CONV_WIDTH + 1, LANES), lambda b, s: (0, s)),
        ],
        out_specs=pl.BlockSpec((CONV_BLOCK, LANES), lambda b, s: (b, s)),
        out_shape=jax.ShapeDtypeStruct((n_tok, d), BF16),
        scratch_shapes=[pltpu.VMEM((pad_rows, LANES), F32)],
        compiler_params=pltpu.CompilerParams(
            dimension_semantics=("arbitrary", "arbitrary"), vmem_limit_bytes=VMEM_LIMIT_BYTES),
        name="dwconv",
    )(u, wpad)


def _gelu(v):
    return 0.5 * v * (1.0 + lax.erf(v * (1.0 / math.sqrt(2.0))))


def _cmlp_kernel(x_ref, shift_ref, scale_ref, gate_ref, nw_ref, win_ref, bin_ref, lw_ref, lb_ref, spw_ref, spb_ref,
                 wout_ref, bout_ref, o_ref, u_ref, v_ref):
    x = x_ref[...]
    tm = x.shape[0]
    hb = _modnorm(x, nw_ref[...], shift_ref[...], scale_ref[...]).astype(BF16)
    for j in range(CMLP_E // FFN_CHUNK):
        cu = slice(j * FFN_CHUNK, (j + 1) * FFN_CHUNK)
        cv = slice(CMLP_E + j * FFN_CHUNK, CMLP_E + (j + 1) * FFN_CHUNK)
        u_ref[:, cu] = _gelu(jnp.dot(hb, win_ref[:, cu], preferred_element_type=F32) + bin_ref[:, cu]).astype(BF16)
        v_ref[:, cu] = _gelu(jnp.dot(hb, win_ref[:, cv], preferred_element_type=F32) + bin_ref[:, cv])
    v = v_ref[...]
    mu = jnp.mean(v, axis=-1, keepdims=True)
    vc = v - mu
    var = jnp.mean(vc * vc, axis=-1, keepdims=True)
    v_ref[...] = vc * lax.rsqrt(var + EPS) * lw_ref[...] + lb_ref[...]
    for c in range(tm // CHUNK):
        rows = slice(c * CHUNK, (c + 1) * CHUNK)
        for g in range(CMLP_GROUPS):
            cols = slice(g * CMLP_GD, (g + 1) * CMLP_GD)
            sv = jnp.dot(spw_ref[g], v_ref[rows, cols].astype(BF16), preferred_element_type=F32) + spb_ref[g]
            u_ref[rows, cols] = (u_ref[rows, cols].astype(F32) * sv).astype(BF16)
    y = jnp.dot(u_ref[...], wout_ref[...], preferred_element_type=F32) + bout_ref[...]
    o_ref[...] = x + gate_ref[...] * y


def cmlp_layer(xs, n_tok, seq, n_batch, mods, nw, in_w, in_b, ln_w, ln_b, sp_w, sp_b, out_w, out_b):
    d = xs.shape[-1]
    tm = TOKEN_TM
    tpb = seq // tm
    const = lambda i: (0, 0)
    const3 = lambda i: (0, 0, 0)
    spb = jnp.broadcast_to(sp_b[:, :, None], (CMLP_GROUPS, CHUNK, CMLP_GD))
    return pl.pallas_call(
        _cmlp_kernel,
        grid=(n_tok // tm,),
        in_specs=[
            pl.BlockSpec((tm, d), lambda i: (i, 0)),
            _mod_spec(3, tpb, n_batch), _mod_spec(4, tpb, n_batch), _mod_spec(5, tpb, n_batch),
            pl.BlockSpec((1, d), const),
            pl.BlockSpec((d, 2 * CMLP_E), const, pipeline_mode=pl.Buffered(1)),
            pl.BlockSpec((1, 2 * CMLP_E), const),
            pl.BlockSpec((1, CMLP_E), const), pl.BlockSpec((1, CMLP_E), const),
            pl.BlockSpec((CMLP_GROUPS, CHUNK, CHUNK), const3),
            pl.BlockSpec((CMLP_GROUPS, CHUNK, CMLP_GD), const3, pipeline_mode=pl.Buffered(1)),
            pl.BlockSpec((CMLP_E, d), const, pipeline_mode=pl.Buffered(1)),
            pl.BlockSpec((1, d), const),
        ],
        out_specs=pl.BlockSpec((tm, d), lambda i: (i, 0)),
        out_shape=jax.ShapeDtypeStruct((n_tok, d), F32),
        scratch_shapes=[pltpu.VMEM((tm, CMLP_E), BF16), pltpu.VMEM((tm, CMLP_E), F32)],
        compiler_params=pltpu.CompilerParams(
            dimension_semantics=("arbitrary",), vmem_limit_bytes=VMEM_LIMIT_BYTES),
        name="cmlp_layer",
    )(xs, mods, mods, mods, nw.reshape(1, d), in_w.astype(BF16), in_b.reshape(1, 2 * CMLP_E),
      ln_w.reshape(1, CMLP_E), ln_b.reshape(1, CMLP_E), sp_w.astype(BF16), spb,
      out_w.astype(BF16), out_b.reshape(1, d))


def kernel(x, c, ctx, c_ctx, ada_w, ada_b, norm_w, ffn_w_in, ffn_w_out, conv_pw1_w, conv_pw1_b, conv_dw_w, conv_dw_b, conv_ln_w, conv_ln_b, conv_pw2_w, conv_pw2_b, ssd_in_w, ssd_conv_w, ssd_conv_b, ssd_dt_bias, ssd_a_log, ssd_d, ssd_norm_w, ssd_out_w, cmlp_in_w, cmlp_in_b, cmlp_ln_w, cmlp_ln_b, cmlp_sp_w, cmlp_sp_b, cmlp_out_w, cmlp_out_b, final_norm_w):
    bsz, seq, d = x.shape
    clen = ctx.shape[1]
    n_lat = bsz * seq
    n_all = n_lat + bsz * clen
    last_ssd = max((i for i in range(DEPTH) if i % N_MIXERS == 1), default=-1)

    cvec = jnp.concatenate([c, c_ctx[None, :], jnp.zeros((MOD_ROWS - bsz - 1, d), F32)], axis=0)
    mods_all = ada_modulation(cvec, ada_w, ada_b)
    w_in_b = ffn_w_in.astype(BF16)
    w_out_b = ffn_w_out.astype(BF16)

    xs, xs_ctx = x.reshape(n_lat, d), ctx.reshape(bsz * clen, d)
    for i in range(DEPTH):
        kind, j = i % N_MIXERS, i // N_MIXERS
        ctx_live = i <= last_ssd
        ctx_carry = i < last_ssd
        n_mix = n_all if ctx_carry else n_lat
        mods = mods_all[i].reshape(MOD_ROWS * N_MODS, 1, d)

        n_in = n_all if ctx_live else n_lat
        glu = (norm_w[i, 1], conv_pw1_w[j].astype(BF16), conv_pw1_b[j]) if kind == 0 else None
        res = ffn_block(xs, n_in, mods, 0, norm_w[i, 0], w_in_b[i, 0], w_out_b[i, 0], seq, bsz,
                        xs2=xs_ctx if (i == 0 and ctx_live) else None, glu=glu)
        mix = None
        if kind == 0:
            assert n_in == n_mix and seq == CONV_BLOCK and CONV_BLOCK % clen == 0 and (n_mix - n_lat) % CONV_BLOCK == 0
            xs, u = res
            v = dwconv(u, conv_dw_w[j], n_lat // CONV_BLOCK, clen if n_mix > n_lat else None)
            mix = (v, conv_pw2_w[j].astype(BF16), conv_pw2_b[j], conv_dw_b[j], conv_ln_w[j], conv_ln_b[j])
        elif kind == 1:
            assert ctx_live and not ctx_carry, "context is only carried up to the last SSD layer"
            xs = res
            yn = ssd_mixer(xs, n_lat, n_all - n_lat, seq, clen, bsz, mods, norm_w[i, 1], ssd_in_w[j], ssd_conv_w[j],
                           ssd_conv_b[j], ssd_dt_bias[j], ssd_a_log[j], ssd_d[j], ssd_norm_w[j])
            mix = (yn, ssd_out_w[j].astype(BF16), jnp.zeros((d,), F32))
        else:
            xs = cmlp_layer(res, n_mix, seq, bsz, mods, norm_w[i, 1], cmlp_in_w[j], cmlp_in_b[j], cmlp_ln_w[j],
                            cmlp_ln_b[j], cmlp_sp_w[j], cmlp_sp_b[j], cmlp_out_w[j], cmlp_out_b[j])
        xs = ffn_block(xs, n_mix, mods, 6, norm_w[i, 2], w_in_b[i, 1], w_out_b[i, 1], seq, bsz, mix=mix,
                       final_w=final_norm_w if i == DEPTH - 1 else None)
    return xs.reshape(bsz, seq, d)
```

```python
import functools
import math

import jax
import jax.numpy as jnp
from jax import lax
from jax.experimental import pallas as pl
from jax.experimental.pallas import tpu as pltpu

D_MODEL = 1024
DEPTH = 4
GRID_W = 64
N_MIXERS = 3
N_MODS = 9
EPS = 1e-6
D_FF = 2816
CONV_WIDTH = 31
D_INNER = 2 * D_MODEL
SSD_HEADDIM = 64
SSD_HEADS = D_INNER // SSD_HEADDIM
SSD_GROUPS = 8
SSD_HPG = SSD_HEADS // SSD_GROUPS
D_STATE = 128
SSD_CONV = 7
SSD_CHUNK = 128
SSD_GN = SSD_GROUPS * D_STATE
SSD_CONV_DIM = D_INNER + 2 * SSD_GN
CHUNK = 128
CMLP_E = 2 * D_MODEL
CMLP_GROUPS = 8
CMLP_GD = CMLP_E // CMLP_GROUPS

SUBLANES = 8
LANES = 128
VMEM_LIMIT_BYTES = 56 * 1024 * 1024

MOD_ROWS = 24
FFN_TM = 1024
FFN_TM_FUSED = 512
TOKEN_TM = 512
FFN_CHUNK = 256
BF16 = jnp.bfloat16
F32 = jnp.float32


def _silu(v):
    return v * jax.nn.sigmoid(v)


def _ada_kernel(c_ref, w_ref, b_ref, o_ref):
    s = _silu(c_ref[...]).astype(BF16)
    o_ref[...] = jnp.dot(s, w_ref[...].astype(BF16), preferred_element_type=F32) + b_ref[...]


def ada_modulation(cvec, ada_w, ada_b):
    depth, d, n = ada_w.shape
    tn = d
    return pl.pallas_call(
        _ada_kernel,
        grid=(depth, n // tn),
        in_specs=[
            pl.BlockSpec((MOD_ROWS, d), lambda l, j: (0, 0)),
            pl.BlockSpec((None, d, tn), lambda l, j: (l, 0, j)),
            pl.BlockSpec((None, 1, tn), lambda l, j: (l, 0, j)),
        ],
        out_specs=pl.BlockSpec((None, MOD_ROWS, tn), lambda l, j: (l, 0, j)),
        out_shape=jax.ShapeDtypeStruct((depth, MOD_ROWS, n), F32),
        compiler_params=pltpu.CompilerParams(
            dimension_semantics=("arbitrary", "arbitrary"), vmem_limit_bytes=VMEM_LIMIT_BYTES),
        name="ada_modulation",
    )(cvec, ada_w, ada_b.reshape(depth, 1, n))


def _modnorm(x, nw, shift, scale):
    xn = x * lax.rsqrt(jnp.mean(x * x, axis=-1, keepdims=True) + EPS) * nw
    return xn * (1.0 + scale) + shift


def _ffn_kernel(*refs, names, n_x_tiles):
    r = dict(zip(names, refs))
    x = r["x"][...]
    if "x2" in r:
        x = jnp.where(pl.program_id(0) < n_x_tiles, x, r["x2"][...])
    if "mix_a" in r:
        a = r["mix_a"][...]
        if "mix_lw" in r:
            v = a.astype(F32) + r["mix_cb"][...]
            vc = v - jnp.mean(v, axis=-1, keepdims=True)
            var = jnp.mean(vc * vc, axis=-1, keepdims=True)
            a = _silu(vc * lax.rsqrt(var + EPS) * r["mix_lw"][...] + r["mix_lb"][...]).astype(BF16)
        y0 = jnp.dot(a, r["mix_w"][...], preferred_element_type=F32) + r["mix_b"][...]
        x = x + r["mix_gate"][...] * y0
    hb = _modnorm(x, r["nw"][...], r["shift"][...], r["scale"][...]).astype(BF16)
    win_ref, wout_ref, a_ref = r["win"], r["wout"], r["a_scr"]
    for j in range(D_FF // FFN_CHUNK):
        lo = j * FFN_CHUNK
        g = jnp.dot(hb, win_ref[:, lo:lo + FFN_CHUNK], preferred_element_type=F32)
        u = jnp.dot(hb, win_ref[:, D_FF + lo:D_FF + lo + FFN_CHUNK], preferred_element_type=F32)
        a_ref[:, lo:lo + FFN_CHUNK] = (_silu(g) * u).astype(BF16)
    y = jnp.dot(a_ref[...], wout_ref[...], preferred_element_type=F32)
    out = x + (0.5 * r["gate"][...]) * y
    if "glu_w" in r:
        d = out.shape[-1]
        h2 = _modnorm(out, r["glu_nw"][...], r["glu_shift"][...], r["glu_scale"][...]).astype(BF16)
        for j in range(d // FFN_CHUNK):
            ca = slice(j * FFN_CHUNK, (j + 1) * FFN_CHUNK)
            cg = slice(d + j * FFN_CHUNK, d + (j + 1) * FFN_CHUNK)
            ga = jnp.dot(h2, r["glu_w"][:, ca], preferred_element_type=F32) + r["glu_b"][:, ca]
            gg = jnp.dot(h2, r["glu_w"][:, cg], preferred_element_type=F32) + r["glu_b"][:, cg]
            r["u"][:, ca] = (ga * jax.nn.sigmoid(gg)).astype(BF16)
    if "fw" in r:
        out = out * lax.rsqrt(jnp.mean(out * out, axis=-1, keepdims=True) + EPS) * r["fw"][...]
    r["o"][...] = out


def _mod_spec(m, tiles_per_batch, n_batch):
    return pl.BlockSpec(
        (None, 1, D_MODEL),
        lambda i: (jnp.minimum(i // tiles_per_batch, n_batch) * N_MODS + m, 0, 0))


def ffn_block(xs, n_tokens, mods, m0, nw, w_in, w_out, seq, n_batch, xs2=None, mix=None, glu=None, final_w=None):
    d = xs.shape[-1]
    tm = FFN_TM if (mix is None and glu is None) else FFN_TM_FUSED
    tpb = seq // tm
    n1 = xs.shape[0] // tm
    assert seq % tm == 0 and n_tokens % tm == 0 and xs.shape[0] % tm == 0, "token counts must be whole tiles"
    const = lambda i: (0, 0)
    vec = lambda n: pl.BlockSpec((1, n), const)
    tile = lambda n: pl.BlockSpec((tm, n), lambda i: (i, 0))
    resident = lambda shape: pl.BlockSpec(shape, const, pipeline_mode=pl.Buffered(1))
    mod = lambda m: _mod_spec(m, tpb, n_batch)
    ins = []
    if xs2 is None:
        ins.append(("x", tile(d), xs))
    else:
        ins.append(("x", pl.BlockSpec((tm, d), lambda i: (jnp.minimum(i, n1 - 1), 0)), xs))
        ins.append(("x2", pl.BlockSpec((tm, d), lambda i: (jnp.maximum(i - n1, 0), 0)), xs2))
    if mix is not None:
        a, w, b = mix[:3]
        ins += [("mix_a", tile(a.shape[1]), a), ("mix_gate", mod(5), mods),
                ("mix_w", resident(w.shape), w), ("mix_b", vec(d), b.reshape(1, d))]
        if len(mix) > 3:
            ins += [(k, vec(a.shape[1]), t.reshape(1, -1)) for k, t in zip(("mix_cb", "mix_lw", "mix_lb"), mix[3:])]
    ins += [("shift", mod(m0), mods), ("scale", mod(m0 + 1), mods), ("gate", mod(m0 + 2), mods),
            ("nw", vec(d), nw.reshape(1, d)),
            ("win", resident((d, 2 * D_FF)), w_in), ("wout", resident((D_FF, d)), w_out)]
    outs = [("o", tile(d), jax.ShapeDtypeStruct((n_tokens, d), F32))]
    if glu is not None:
        gnw, gw, gb = glu
        ins += [("glu_shift", mod(3), mods), ("glu_scale", mod(4), mods), ("glu_nw", vec(d), gnw.reshape(1, d)),
                ("glu_w", resident(gw.shape), gw), ("glu_b", vec(gw.shape[1]), gb.reshape(1, -1))]
        outs.append(("u", tile(d), jax.ShapeDtypeStruct((n_tokens, d), BF16)))
    if final_w is not None:
        ins.append(("fw", vec(d), final_w.reshape(1, d)))
    names = tuple(n for n, _, _ in ins) + tuple(n for n, _, _ in outs) + ("a_scr",)
    res = pl.pallas_call(
        functools.partial(_ffn_kernel, names=names, n_x_tiles=n1),
        grid=(n_tokens // tm,),
        in_specs=[s for _, s, _ in ins],
        out_specs=[s for _, s, _ in outs],
        out_shape=[s for _, _, s in outs],
        scratch_shapes=[pltpu.VMEM((tm, D_FF), BF16)],
        compiler_params=pltpu.CompilerParams(
            dimension_semantics=("arbitrary",), vmem_limit_bytes=VMEM_LIMIT_BYTES),
        name="ffn_block",
    )(*[t for _, _, t in ins])
    return res[0] if glu is None else tuple(res)


SSD_HALO = SUBLANES
SSD_NEG = -1e30
SSD_XBC_CHUNK = 512
SSD_STEP_CHUNKS = 2
LOG2E = 1.4426950408889634
HP = D_INNER
GHP = D_INNER // SSD_GROUPS


def _softplus(v):
    return jnp.maximum(v, 0.0) + jnp.log1p(jnp.exp(-jnp.abs(v)))


def _ssd_proj_kernel(xp_ref, x_ref, xn_ref, shift_ref, scale_ref, nw_ref, wz_ref, wdt_ref, wxbc_ref,
                     cw_ref, cb_ref, dtb_ref, z_ref, dt_ref, xbc_ref, ext_ref, *, tiles_per_seq, tm):
    pos = pl.program_id(0) % tiles_per_seq
    nw, shift, scale = nw_ref[...], shift_ref[...], scale_ref[...]
    h = _modnorm(x_ref[...], nw, shift, scale)
    hp = _modnorm(xp_ref[...], nw, shift, scale) * (pos > 0).astype(F32)
    hn = _modnorm(xn_ref[...], nw, shift, scale) * (pos < tiles_per_seq - 1).astype(F32)
    hb = h.astype(BF16)
    hext = jnp.concatenate([hp, h, hn], axis=0).astype(BF16)

    for j in range(D_INNER // SSD_XBC_CHUNK):
        cols = slice(j * SSD_XBC_CHUNK, (j + 1) * SSD_XBC_CHUNK)
        z_ref[:, cols] = jnp.dot(hb, wz_ref[:, cols], preferred_element_type=F32).astype(BF16)
    dt_ref[...] = _softplus(jnp.dot(hb, wdt_ref[...], preferred_element_type=F32) + dtb_ref[...])

    off = SSD_HALO - SSD_CONV // 2
    slabs_per_chunk = SSD_XBC_CHUNK // LANES
    for j in range(SSD_CONV_DIM // SSD_XBC_CHUNK):
        cols = slice(j * SSD_XBC_CHUNK, (j + 1) * SSD_XBC_CHUNK)
        e = jnp.dot(hext, wxbc_ref[:, cols], preferred_element_type=F32)
        for s in range(slabs_per_chunk):
            sl = j * slabs_per_chunk + s
            lc = slice(sl * LANES, (sl + 1) * LANES)
            ext_ref[sl] = e[:, s * LANES:(s + 1) * LANES]
            acc = ext_ref[sl, pl.ds(off, tm, stride=1), :] * cw_ref[0:1, lc] + cb_ref[:, lc]
            for k in range(1, SSD_CONV):
                acc = acc + ext_ref[sl, pl.ds(off + k, tm, stride=1), :] * cw_ref[k:k + 1, lc]
            xbc_ref[:, lc] = _silu(acc).astype(BF16)


def ssd_proj(xs, tok0, n_tok, seq, tm, mods, mod_row, nw, w_z, w_dt, w_xbc, conv_w, conv_b, dt_bias128):
    d = xs.shape[-1]
    tps = seq // tm
    t0 = tok0 // tm
    hb = tm // SSD_HALO
    n_hblk = xs.shape[0] // SSD_HALO
    const = lambda i: (0, 0)
    mspec = lambda m: pl.BlockSpec((None, 1, d), lambda i: (mod_row(i) * N_MODS + m, 0, 0))
    return pl.pallas_call(
        functools.partial(_ssd_proj_kernel, tiles_per_seq=tps, tm=tm),
        grid=(n_tok // tm,),
        in_specs=[
            pl.BlockSpec((SSD_HALO, d), lambda i: (jnp.maximum((i + t0) * hb - 1, 0), 0)),
            pl.BlockSpec((tm, d), lambda i: (i + t0, 0)),
            pl.BlockSpec((SSD_HALO, d), lambda i: (jnp.minimum((i + t0 + 1) * hb, n_hblk - 1), 0)),
            mspec(3), mspec(4),
            pl.BlockSpec((1, d), const),
            pl.BlockSpec((d, D_INNER), const, pipeline_mode=pl.Buffered(1)),
            pl.BlockSpec((d, 128), const, pipeline_mode=pl.Buffered(1)),
            pl.BlockSpec((d, SSD_CONV_DIM), const, pipeline_mode=pl.Buffered(1)),
            pl.BlockSpec((SSD_CONV + 1, SSD_CONV_DIM), const),
            pl.BlockSpec((1, SSD_CONV_DIM), const),
            pl.BlockSpec((1, 128), const),
        ],
        out_specs=[
            pl.BlockSpec((tm, D_INNER), lambda i: (i, 0)),
            pl.BlockSpec((tm, 128), lambda i: (i, 0)),
            pl.BlockSpec((tm, SSD_CONV_DIM), lambda i: (i, 0)),
        ],
        out_shape=[
            jax.ShapeDtypeStruct((n_tok, D_INNER), BF16),
            jax.ShapeDtypeStruct((n_tok, 128), F32),
            jax.ShapeDtypeStruct((n_tok, SSD_CONV_DIM), BF16),
        ],
        scratch_shapes=[pltpu.VMEM((SSD_CONV_DIM // LANES, tm + 2 * SSD_HALO, LANES), F32)],
        compiler_params=pltpu.CompilerParams(
            dimension_semantics=("arbitrary",), vmem_limit_bytes=VMEM_LIMIT_BYTES),
        name="ssd_proj",
    )(xs, xs, xs, mods, mods, nw.reshape(1, d), w_z, w_dt, w_xbc,
      jnp.concatenate([conv_w, jnp.zeros((1, SSD_CONV_DIM), F32)], axis=0), conv_b.reshape(1, SSD_CONV_DIM),
      dt_bias128)


def _split3(v):
    hi = v.astype(BF16)
    r = v - hi.astype(F32)
    mid = r.astype(BF16)
    lo = (r - mid.astype(F32)).astype(BF16)
    return hi, mid, lo


def _sum3(p, n, axis):
    if axis == 1:
        return p[:, :n] + p[:, n:2 * n] + p[:, 2 * n:]
    return p[:n] + p[n:2 * n] + p[2 * n:]


def _chunk_decays(dt128, alog128, with_rows):
    q = dt128.shape[0]
    ri = lax.broadcasted_iota(jnp.int32, (q, q), 0)
    ci = lax.broadcasted_iota(jnp.int32, (q, q), 1)
    ltri = (ci <= ri).astype(BF16)
    utri = (ci >= ri).astype(BF16)
    lane = lax.broadcasted_iota(jnp.int32, (1, LANES), 1)
    lane_fwd = (lane % (2 * SSD_HEADS)) < SSD_HEADS
    v = dt128 * (-jnp.exp(alog128) * LOG2E)
    v3 = jnp.concatenate(_split3(v), axis=1)
    csl = _sum3(jnp.dot(ltri, v3, preferred_element_type=F32), LANES, 1)
    csu = _sum3(jnp.dot(utri, v3, preferred_element_type=F32), LANES, 1)
    cs = jnp.where(lane_fwd, csl, csu)
    edge = jnp.where(lane_fwd, csl[q - 1:q, :], csu[0:1, :])
    qm = jnp.where(lane < 2 * SSD_HEADS, jnp.exp2(cs), jnp.exp2(edge - cs) * dt128)
    if not with_rows:
        return cs, qm, None
    nh = 2 * SSD_HEADS
    vt3 = jnp.concatenate(_split3(v.T[:nh]), axis=0)
    p = _sum3(jnp.dot(vt3, jnp.concatenate([utri, ltri], axis=1), preferred_element_type=F32), nh, 0)
    row = lax.broadcasted_iota(jnp.int32, (nh, 1), 0)
    cst = jnp.where(row < SSD_HEADS, p[:, :q], p[:, q:])
    return cs, qm, cst - jnp.log2(dt128.T[:nh])


def _split2(qm):
    hi = qm.astype(BF16)
    lo = (qm - hi.astype(F32)).astype(BF16)
    return jnp.concatenate([hi, lo], axis=1)


def _expand_heads(qq, e_mat):
    return jnp.dot(qq, e_mat, preferred_element_type=F32)


def _group_state(bg, xg, wg):
    return lax.dot_general(bg, xg * wg.astype(BF16), (((0,), (0,)), ((), ())), preferred_element_type=F32)


def _ssd_states_kernel(xf_ref, bf_ref, dtf_ref, xb_ref, bb_ref, dtb_ref, alog_ref, e_ref, hf_ref, hb_ref):
    s = pl.program_id(1)

    @pl.when(s == 0)
    def _():
        hf_ref[...] = jnp.zeros_like(hf_ref)
        hb_ref[...] = jnp.zeros_like(hb_ref)

    for x_ref, b_ref, dt_ref, st_ref, k_w, k_dec, edge_row in (
            (xf_ref, bf_ref, dtf_ref, hf_ref, 2, 0, SSD_CHUNK - 1), (xb_ref, bb_ref, dtb_ref, hb_ref, 3, 1, 0)):
        _, qm, _ = _chunk_decays(dt_ref[...], alog_ref[...], False)
        qq = _split2(qm)
        w = _expand_heads(qq, e_ref[k_w])
        dec = _expand_heads(qq, e_ref[k_dec])[edge_row:edge_row + 1, :]
        x = x_ref[...]
        bm = b_ref[...]
        for g in range(SSD_GROUPS):
            gl = slice(g * GHP, (g + 1) * GHP)
            st = _group_state(bm[:, g * D_STATE:(g + 1) * D_STATE], x[:, gl], w[:, gl])
            st_ref[g] = st_ref[g] * dec[:, gl] + st


def ssd_context_states(xbc, dt, alog128, e_all, n_batch, seq):
    nc = seq // SSD_CHUNK
    q = SSD_CHUNK
    st_shape = jax.ShapeDtypeStruct((n_batch, SSD_GROUPS, D_STATE, GHP), F32)
    st_spec = pl.BlockSpec((None, SSD_GROUPS, D_STATE, GHP), lambda b, s: (b, 0, 0, 0))
    fwd = lambda b, s: b * nc + s
    bwd = lambda b, s: b * nc + nc - 1 - s
    return pl.pallas_call(
        _ssd_states_kernel,
        grid=(n_batch, nc),
        in_specs=[
            pl.BlockSpec((q, HP), lambda b, s: (fwd(b, s), 0)),
            pl.BlockSpec((q, SSD_GN), lambda b, s: (fwd(b, s), HP // SSD_GN)),
            pl.BlockSpec((q, 128), lambda b, s: (fwd(b, s), 0)),
            pl.BlockSpec((q, HP), lambda b, s: (bwd(b, s), 0)),
            pl.BlockSpec((q, SSD_GN), lambda b, s: (bwd(b, s), HP // SSD_GN)),
            pl.BlockSpec((q, 128), lambda b, s: (bwd(b, s), 0)),
            pl.BlockSpec((1, 128), lambda b, s: (0, 0)),
            pl.BlockSpec((4, 256, HP), lambda b, s: (0, 0, 0), pipeline_mode=pl.Buffered(1)),
        ],
        out_specs=[st_spec, st_spec],
        out_shape=[st_shape, st_shape],
        compiler_params=pltpu.CompilerParams(
            dimension_semantics=("arbitrary", "arbitrary"), vmem_limit_bytes=VMEM_LIMIT_BYTES),
        name="ssd_context_states",
    )(xbc, xbc, dt, xbc, xbc, dt, alog128, e_all)


def _ssd_bwd_chunk(x, bm, dt128, alog128, e_ref, hb_ref, hbs_ref, cc):
    _, qm, _ = _chunk_decays(dt128, alog128, False)
    qq = _split2(qm)
    w = _expand_heads(qq, e_ref[3])
    dec = _expand_heads(qq[0:2 * SUBLANES], e_ref[1])[0:1, :]
    for g in range(SSD_GROUPS):
        gl = slice(g * GHP, (g + 1) * GHP)
        hbs_ref[cc, g] = hb_ref[g].astype(BF16)
        st = _group_state(bm[:, g * D_STATE:(g + 1) * D_STATE], x[:, gl], w[:, gl])
        hb_ref[g] = hb_ref[g] * dec[:, gl] + st


def _ssd_fwd_chunk(x, bm, cm, z, dt128, alog128, dskip, normw, e_ref, hf_ref, hbs_ref, cc, o_ref, rows):
    q = SSD_CHUNK
    cs, qm, cst = _chunk_decays(dt128, alog128, True)
    qq = _split2(qm)
    ef = _expand_heads(qq, e_ref[0])
    eb = _expand_heads(qq, e_ref[1])
    wf = _expand_heads(qq, e_ref[2])
    dec = ef[q - 1:q, :]
    ri = lax.broadcasted_iota(jnp.int32, (q, q), 0)
    ci = lax.broadcasted_iota(jnp.int32, (q, q), 1)
    tril = ci <= ri
    triu = ci >= ri
    glane = lax.broadcasted_iota(jnp.int32, (1, GHP), 1) // SSD_HEADDIM
    for g in range(SSD_GROUPS):
        gl = slice(g * GHP, (g + 1) * GHP)
        bg = bm[:, g * D_STATE:(g + 1) * D_STATE]
        cg = cm[:, g * D_STATE:(g + 1) * D_STATE]
        xg = x[:, gl]
        cb = lax.dot_general(cg, bg, (((1,), (1,)), ((), ())), preferred_element_type=F32)
        ms, xs = [], []
        for r in range(SSD_HPG):
            hd = g * SSD_HPG + r
            hb_ = SSD_HEADS + hd
            lf = jnp.exp2(jnp.where(tril, cs[:, hd:hd + 1] - cst[hd:hd + 1, :], SSD_NEG))
            lb = jnp.exp2(jnp.where(triu, cs[:, hb_:hb_ + 1] - cst[hb_:hb_ + 1, :], SSD_NEG))
            ms.append((cb * (lf + lb)).astype(BF16))
            xs.append(jnp.where(glane == r, xg, jnp.zeros_like(xg)))
        y = jnp.dot(jnp.concatenate(ms, axis=1), jnp.concatenate(xs, axis=0), preferred_element_type=F32)
        y = y + jnp.dot(cg, hf_ref[g].astype(BF16), preferred_element_type=F32) * ef[:, gl]
        y = y + jnp.dot(cg, hbs_ref[cc, g], preferred_element_type=F32) * eb[:, gl]
        y = y + dskip[:, gl] * xg.astype(F32)
        yz = y * _silu(z[:, gl].astype(F32))
        yn = yz * lax.rsqrt(jnp.mean(yz * yz, axis=-1, keepdims=True) + EPS) * normw[:, gl]
        o_ref[rows, gl] = yn.astype(BF16)
        hf_ref[g] = hf_ref[g] * dec[:, gl] + _group_state(bg, xg, wf[:, gl])


def _ssd_scan_kernel(x_ref, b_ref, c_ref, z_ref, dt_ref, alog_ref, dskip_ref, normw_ref, e_ref, h0f_ref, h0b_ref,
                     o_ref, hf_ref, hb_ref, hbs_ref, *, n_steps):
    s = pl.program_id(1)
    q = SSD_CHUNK

    @pl.when(s == 0)
    def _():
        hf_ref[...] = h0f_ref[...]
        hb_ref[...] = h0b_ref[...]

    @pl.when(s < n_steps)
    def _():
        blk = n_steps - 1 - s
        for j in reversed(range(SSD_STEP_CHUNKS)):
            rows = slice(j * q, (j + 1) * q)
            _ssd_bwd_chunk(x_ref[rows, :], b_ref[rows, :], dt_ref[rows, :], alog_ref[...], e_ref, hb_ref, hbs_ref,
                           blk * SSD_STEP_CHUNKS + j)

    @pl.when(s >= n_steps)
    def _():
        blk = s - n_steps
        for j in range(SSD_STEP_CHUNKS):
            rows = slice(j * q, (j + 1) * q)
            _ssd_fwd_chunk(x_ref[rows, :], b_ref[rows, :], c_ref[rows, :], z_ref[rows, :], dt_ref[rows, :],
                           alog_ref[...], dskip_ref[...], normw_ref[...], e_ref, hf_ref, hbs_ref,
                           blk * SSD_STEP_CHUNKS + j, o_ref, rows)


def ssd_scan(xbc, z, dt, alog128, dskip_row, norm_w, e_all, h0f, h0b, n_batch, seq):
    nc = seq // SSD_CHUNK
    q = SSD_CHUNK * SSD_STEP_CHUNKS
    ns = seq // q
    chunk = lambda b, s: b * ns + jnp.where(s < ns, ns - 1 - s, s - ns)
    chunk_b = lambda b, s: b * ns + jnp.maximum(s - ns, 0)
    st_spec = pl.BlockSpec((None, SSD_GROUPS, D_STATE, GHP), lambda b, s: (b, 0, 0, 0))
    row = lambda n: pl.BlockSpec((1, n), lambda b, s: (0, 0))
    return pl.pallas_call(
        functools.partial(_ssd_scan_kernel, n_steps=ns),
        grid=(n_batch, 2 * ns),
        in_specs=[
            pl.BlockSpec((q, HP), lambda b, s: (chunk(b, s), 0)),
            pl.BlockSpec((q, SSD_GN), lambda b, s: (chunk(b, s), HP // SSD_GN)),
            pl.BlockSpec((q, SSD_GN), lambda b, s: (chunk_b(b, s), HP // SSD_GN + 1)),
            pl.BlockSpec((q, HP), lambda b, s: (chunk_b(b, s), 0)),
            pl.BlockSpec((q, 128), lambda b, s: (chunk(b, s), 0)),
            row(128), row(HP), row(HP),
            pl.BlockSpec((4, 256, HP), lambda b, s: (0, 0, 0), pipeline_mode=pl.Buffered(1)),
            st_spec, st_spec,
        ],
        out_specs=pl.BlockSpec((q, HP), lambda b, s: (chunk_b(b, s), 0)),
        out_shape=jax.ShapeDtypeStruct((n_batch * seq, HP), BF16),
        scratch_shapes=[
            pltpu.VMEM((SSD_GROUPS, D_STATE, GHP), F32),
            pltpu.VMEM((SSD_GROUPS, D_STATE, GHP), F32),
            pltpu.VMEM((nc, SSD_GROUPS, D_STATE, GHP), BF16),
        ],
        compiler_params=pltpu.CompilerParams(
            dimension_semantics=("arbitrary", "arbitrary"), vmem_limit_bytes=VMEM_LIMIT_BYTES),
        name="ssd_scan",
    )(xbc, xbc, xbc, z, dt, alog128, dskip_row, norm_w.reshape(1, HP), e_all, h0f, h0b)


def _ssd_expand_matrices():
    hd = jnp.arange(HP) // SSD_HEADDIM
    rows = jnp.arange(256)
    mats = []
    for k in range(4):
        sel = ((rows % 128) // SSD_HEADS == k)
        mats.append(((rows % SSD_HEADS)[:, None] == hd[None, :]) & sel[:, None])
    return jnp.stack(mats).astype(BF16)


def ssd_mixer(xs, n_lat, n_ctx, seq, clen, n_batch, mods, nw, in_w, conv_w, conv_b, dt_bias, a_log, d_skip, norm_w):
    w_z = in_w[:, :D_INNER].astype(BF16)
    w_dt2 = in_w[:, D_INNER:D_INNER + 2 * SSD_HEADS]
    w_dt = jnp.concatenate([w_dt2, w_dt2], axis=1).astype(BF16)
    w_xbc = in_w[:, D_INNER + 2 * SSD_HEADS:].astype(BF16)
    both = lambda t: jnp.concatenate([t.reshape(1, 2 * SSD_HEADS)] * 2, axis=1)
    dt_bias128, alog128 = both(dt_bias), both(a_log)
    dskip_row = jnp.repeat(d_skip, SSD_HEADDIM).reshape(1, HP)
    e_all = _ssd_expand_matrices()
    proj_args = (mods, nw, w_z, w_dt, w_xbc, conv_w, conv_b, dt_bias128)

    ctm = min(clen, TOKEN_TM)
    _, dtc, xbcc = ssd_proj(xs, n_lat, n_ctx, clen, ctm, proj_args[0], lambda i: n_batch, *proj_args[1:])
    h0f, h0b = ssd_context_states(xbcc, dtc, alog128, e_all, n_batch, clen)
    tpb = seq // TOKEN_TM
    zl, dtl, xbcl = ssd_proj(xs, 0, n_lat, seq, TOKEN_TM, proj_args[0], lambda i: i // tpb, *proj_args[1:])
    return ssd_scan(xbcl, zl, dtl, alog128, dskip_row, norm_w, e_all, h0f, h0b, n_batch, seq)


CONV_BLOCK = 2048
CONV_PAD = 16
CONV_HALF = CONV_WIDTH // 2


def _dwconv_rows(u_ref, w_ref, o_ref, pad_ref, row_len):
    n_rows = CONV_BLOCK // row_len
    pitch = row_len + CONV_PAD
    zeros = jnp.zeros((CONV_PAD, LANES), F32)
    pad_ref[0:CONV_PAD, :] = zeros
    for r in range(n_rows):
        base = CONV_PAD + r * pitch
        pad_ref[base:base + row_len, :] = u_ref[r * row_len:(r + 1) * row_len, :].astype(F32)
        pad_ref[base + row_len:base + pitch, :] = zeros
    piece = min(row_len, 64)

    def body(i, carry):
        r = i // (row_len // piece)
        p = i % (row_len // piece)
        src = CONV_PAD + r * pitch + p * piece - CONV_HALF
        acc = pad_ref[pl.ds(src, piece, stride=1), :] * w_ref[0:1, :]
        for k in range(1, CONV_WIDTH):
            acc = acc + pad_ref[pl.ds(src + k, piece, stride=1), :] * w_ref[k:k + 1, :]
        o_ref[pl.ds(pl.multiple_of(i * piece, piece), piece), :] = acc.astype(BF16)
        return carry

    lax.fori_loop(0, CONV_BLOCK // piece, body, 0, unroll=4)


def _dwconv_cols(u_ref, w_ref, o_ref, pad_ref):
    n_rows = CONV_BLOCK // GRID_W
    pad_ref[0:CONV_BLOCK, :] = u_ref[...].astype(F32)
    for r in range(n_rows):
        acc = None
        for k in range(max(0, CONV_HALF - r), min(CONV_WIDTH, n_rows + CONV_HALF - r)):
            src = (r + k - CONV_HALF) * GRID_W
            term = pad_ref[src:src + GRID_W, :] * w_ref[k:k + 1, :]
            acc = term if acc is None else acc + term
        o_ref[r * GRID_W:(r + 1) * GRID_W, :] = acc.astype(BF16)


def _dwconv_kernel(u_ref, w_ref, o_ref, pad_ref, *, n_img, ctx_len):
    blk = pl.program_id(0)
    slab = pl.program_id(1)
    half_slabs = D_MODEL // LANES // 2

    @pl.when(jnp.logical_and(blk < n_img, slab < half_slabs))
    def _():
        _dwconv_rows(u_ref, w_ref, o_ref, pad_ref, GRID_W)

    @pl.when(jnp.logical_and(blk < n_img, slab >= half_slabs))
    def _():
        _dwconv_cols(u_ref, w_ref, o_ref, pad_ref)

    if ctx_len is not None:
        @pl.when(blk >= n_img)
        def _():
            _dwconv_rows(u_ref, w_ref, o_ref, pad_ref, ctx_len)


def dwconv(u, dw_w, n_img, ctx_len):
    n_tok, d = u.shape
    pad_rows = CONV_PAD + (CONV_BLOCK // GRID_W) * (GRID_W + CONV_PAD)
    wpad = jnp.concatenate([dw_w, jnp.zeros((1, d), F32)], axis=0)
    return pl.pallas_call(
        functools.partial(_dwconv_kernel, n_img=n_img, ctx_len=ctx_len),
        grid=(n_tok // CONV_BLOCK, d // LANES),
        in_specs=[
            pl.BlockSpec((CONV_BLOCK, LANES), lambda b, s: (b, s)),
            pl.BlockSpec((CONV_WIDTH + 1, LANES), lambda b, s: (0, s)),
        ],
        out_specs=pl.BlockSpec((CONV_BLOCK, LANES), lambda b, s: (b, s)),
        out_shape=jax.ShapeDtypeStruct((n_tok, d), BF16),
        scratch_shapes=[pltpu.VMEM((pad_rows, LANES), F32)],
        compiler_params=pltpu.CompilerParams(
            dimension_semantics=("arbitrary", "arbitrary"), vmem_limit_bytes=VMEM_LIMIT_BYTES),
        name="dwconv",
    )(u, wpad)


def _gelu(v):
    return 0.5 * v * (1.0 + lax.erf(v * (1.0 / math.sqrt(2.0))))


def _cmlp_kernel(x_ref, shift_ref, scale_ref, gate_ref, nw_ref, win_ref, bin_ref, lw_ref, lb_ref, spw_ref, spb_ref,
                 wout_ref, bout_ref, o_ref, u_ref, v_ref):
    x = x_ref[...]
    tm = x.shape[0]
    hb = _modnorm(x, nw_ref[...], shift_ref[...], scale_ref[...]).astype(BF16)
    for j in range(CMLP_E // FFN_CHUNK):
        cu = slice(j * FFN_CHUNK, (j + 1) * FFN_CHUNK)
        cv = slice(CMLP_E + j * FFN_CHUNK, CMLP_E + (j + 1) * FFN_CHUNK)
        u_ref[:, cu] = _gelu(jnp.dot(hb, win_ref[:, cu], preferred_element_type=F32) + bin_ref[:, cu]).astype(BF16)
        v_ref[:, cu] = _gelu(jnp.dot(hb, win_ref[:, cv], preferred_element_type=F32) + bin_ref[:, cv])
    v = v_ref[...]
    mu = jnp.mean(v, axis=-1, keepdims=True)
    vc = v - mu
    var = jnp.mean(vc * vc, axis=-1, keepdims=True)
    v_ref[...] = vc * lax.rsqrt(var + EPS) * lw_ref[...] + lb_ref[...]
    for c in range(tm // CHUNK):
        rows = slice(c * CHUNK, (c + 1) * CHUNK)
        for g in range(CMLP_GROUPS):
            cols = slice(g * CMLP_GD, (g + 1) * CMLP_GD)
            sv = jnp.dot(spw_ref[g], v_ref[rows, cols].astype(BF16), preferred_element_type=F32) + spb_ref[g]
            u_ref[rows, cols] = (u_ref[rows, cols].astype(F32) * sv).astype(BF16)
    y = jnp.dot(u_ref[...], wout_ref[...], preferred_element_type=F32) + bout_ref[...]
    o_ref[...] = x + gate_ref[...] * y


def cmlp_layer(xs, n_tok, seq, n_batch, mods, nw, in_w, in_b, ln_w, ln_b, sp_w, sp_b, out_w, out_b):
    d = xs.shape[-1]
    tm = TOKEN_TM
    tpb = seq // tm
    const = lambda i: (0, 0)
    const3 = lambda i: (0, 0, 0)
    spb = jnp.broadcast_to(sp_b[:, :, None], (CMLP_GROUPS, CHUNK, CMLP_GD))
    return pl.pallas_call(
        _cmlp_kernel,
        grid=(n_tok // tm,),
        in_specs=[
            pl.BlockSpec((tm, d), lambda i: (i, 0)),
            _mod_spec(3, tpb, n_batch), _mod_spec(4, tpb, n_batch), _mod_spec(5, tpb, n_batch),
            pl.BlockSpec((1, d), const),
            pl.BlockSpec((d, 2 * CMLP_E), const, pipeline_mode=pl.Buffered(1)),
            pl.BlockSpec((1, 2 * CMLP_E), const),
            pl.BlockSpec((1, CMLP_E), const), pl.BlockSpec((1, CMLP_E), const),
            pl.BlockSpec((CMLP_GROUPS, CHUNK, CHUNK), const3),
            pl.BlockSpec((CMLP_GROUPS, CHUNK, CMLP_GD), const3, pipeline_mode=pl.Buffered(1)),
            pl.BlockSpec((CMLP_E, d), const, pipeline_mode=pl.Buffered(1)),
            pl.BlockSpec((1, d), const),
        ],
        out_specs=pl.BlockSpec((tm, d), lambda i: (i, 0)),
        out_shape=jax.ShapeDtypeStruct((n_tok, d), F32),
        scratch_shapes=[pltpu.VMEM((tm, CMLP_E), BF16), pltpu.VMEM((tm, CMLP_E), F32)],
        compiler_params=pltpu.CompilerParams(
            dimension_semantics=("arbitrary",), vmem_limit_bytes=VMEM_LIMIT_BYTES),
        name="cmlp_layer",
    )(xs, mods, mods, mods, nw.reshape(1, d), in_w.astype(BF16), in_b.reshape(1, 2 * CMLP_E),
      ln_w.reshape(1, CMLP_E), ln_b.reshape(1, CMLP_E), sp_w.astype(BF16), spb,
      out_w.astype(BF16), out_b.reshape(1, d))


def kernel(x, c, ctx, c_ctx, ada_w, ada_b, norm_w, ffn_w_in, ffn_w_out, conv_pw1_w, conv_pw1_b, conv_dw_w, conv_dw_b, conv_ln_w, conv_ln_b, conv_pw2_w, conv_pw2_b, ssd_in_w, ssd_conv_w, ssd_conv_b, ssd_dt_bias, ssd_a_log, ssd_d, ssd_norm_w, ssd_out_w, cmlp_in_w, cmlp_in_b, cmlp_ln_w, cmlp_ln_b, cmlp_sp_w, cmlp_sp_b, cmlp_out_w, cmlp_out_b, final_norm_w):
    bsz, seq, d = x.shape
    clen = ctx.shape[1]
    n_lat = bsz * seq
    n_all = n_lat + bsz * clen
    last_ssd = max((i for i in range(DEPTH) if i % N_MIXERS == 1), default=-1)

    cvec = jnp.concatenate([c, c_ctx[None, :], jnp.zeros((MOD_ROWS - bsz - 1, d), F32)], axis=0)
    mods_all = ada_modulation(cvec, ada_w, ada_b)
    w_in_b = ffn_w_in.astype(BF16)
    w_out_b = ffn_w_out.astype(BF16)

    xs, xs_ctx = x.reshape(n_lat, d), ctx.reshape(bsz * clen, d)
    for i in range(DEPTH):
        kind, j = i % N_MIXERS, i // N_MIXERS
        ctx_live = i <= last_ssd
        ctx_carry = i < last_ssd
        n_mix = n_all if ctx_carry else n_lat
        mods = mods_all[i].reshape(MOD_ROWS * N_MODS, 1, d)

        n_in = n_all if ctx_live else n_lat
        glu = (norm_w[i, 1], conv_pw1_w[j].astype(BF16), conv_pw1_b[j]) if kind == 0 else None
        res = ffn_block(xs, n_in, mods, 0, norm_w[i, 0], w_in_b[i, 0], w_out_b[i, 0], seq, bsz,
                        xs2=xs_ctx if (i == 0 and ctx_live) else None, glu=glu)
        mix = None
        if kind == 0:
            assert n_in == n_mix and seq == CONV_BLOCK and CONV_BLOCK % clen == 0 and (n_mix - n_lat) % CONV_BLOCK == 0
            xs, u = res
            v = dwconv(u, conv_dw_w[j], n_lat // CONV_BLOCK, clen if n_mix > n_lat else None)
            mix = (v, conv_pw2_w[j].astype(BF16), conv_pw2_b[j], conv_dw_b[j], conv_ln_w[j], conv_ln_b[j])
        elif kind == 1:
            assert ctx_live and not ctx_carry, "context is only carried up to the last SSD layer"
            xs = res
            yn = ssd_mixer(xs, n_lat, n_all - n_lat, seq, clen, bsz, mods, norm_w[i, 1], ssd_in_w[j], ssd_conv_w[j],
                           ssd_conv_b[j], ssd_dt_bias[j], ssd_a_log[j], ssd_d[j], ssd_norm_w[j])
            mix = (yn, ssd_out_w[j].astype(BF16), jnp.zeros((d,), F32))
        else:
            xs = cmlp_layer(res, n_mix, seq, bsz, mods, norm_w[i, 1], cmlp_in_w[j], cmlp_in_b[j], cmlp_ln_w[j],
                            cmlp_ln_b[j], cmlp_sp_w[j], cmlp_sp_b[j], cmlp_out_w[j], cmlp_out_b[j])
        xs = ffn_block(xs, n_mix, mods, 6, norm_w[i, 2], w_in_b[i, 1], w_out_b[i, 1], seq, bsz, mix=mix,
                       final_w=final_norm_w if i == DEPTH - 1 else None)
    return xs.reshape(bsz, seq, d)
```
